```python
import jax, jax.numpy as jnp
from jax import lax
import numpy as np

D_MODEL = 1024
BATCH = 8
SEQ = 4096
DEPTH = 1

CHUNK = 64
Q_BLOCK = 128
MIX_WIDTH = D_MODEL
POOL_WIDTH = MIX_WIDTH // 2
POOL_WINDOWS = (2, 4, 8, 16)
POOL_GROUPS = len(POOL_WINDOWS)
POOL_GROUP_DIM = POOL_WIDTH // POOL_GROUPS
ATTN_WIDTH = MIX_WIDTH - POOL_WIDTH
DIFF_HEAD_DIM = 64
V_HEAD_DIM = 2 * DIFF_HEAD_DIM
N_DIFF_HEADS = ATTN_WIDTH // V_HEAD_DIM
IN_WIDTH = POOL_WIDTH + 3 * ATTN_WIDTH
D_FF = 2816
CONV_WIDTH = 3
ROPE_THETA = 10000.0
EPS = 1e-6

kernel_name = "hybrid_pool_diffattn_convffn_block"


def rmsnorm(x, g):
    xf = x.astype(jnp.float32)
    y = xf * lax.rsqrt(jnp.mean(xf * xf, axis=-1, keepdims=True) + EPS)
    return (y * g.astype(jnp.float32)).astype(x.dtype)


def rope_tables(positions, dtype):
    inv_freq = ROPE_THETA ** (-jnp.arange(0, DIFF_HEAD_DIM, 2, dtype=jnp.float32) / DIFF_HEAD_DIM)
    ang = positions.astype(jnp.float32)[..., None] * inv_freq
    return (jnp.cos(ang)[:, :, None, None, :].astype(dtype),
            jnp.sin(ang)[:, :, None, None, :].astype(dtype))


def apply_rope(t, cos, sin):
    t1, t2 = jnp.split(t, 2, axis=-1)
    return jnp.concatenate([t1 * cos - t2 * sin, t1 * sin + t2 * cos], axis=-1)


def multiscale_pool(p, pool_w, pool_scale):
    s = p.shape[1]
    pf = p.astype(jnp.float32)
    csum = jnp.pad(jnp.cumsum(pf, axis=1), ((0, 0), (1, 0), (0, 0)))
    t = jnp.arange(s)
    pooled = []
    for g, w in enumerate(POOL_WINDOWS):
        cs = csum[..., g * POOL_GROUP_DIM:(g + 1) * POOL_GROUP_DIM]
        upper = cs[:, 1:]
        lower = jnp.pad(cs[:, :s + 1 - w], ((0, 0), (w - 1, 0), (0, 0)))
        count = jnp.minimum(t + 1, w).astype(jnp.float32)[None, :, None]
        pooled.append((upper - lower) / count)
    d = (jnp.concatenate(pooled, axis=-1) - pf).astype(p.dtype)
    d = d.reshape(p.shape[0], s, POOL_GROUPS, POOL_GROUP_DIM)
    y = jnp.einsum('bsgc,gcd->bsgd', d, pool_w).reshape(p.shape[0], s, POOL_WIDTH)
    return y * pool_scale


def diff_attention(q, k, v, lam, lam_init, head_g):
    b, h, _, s, _ = q.shape
    scale = DIFF_HEAD_DIM ** -0.5
    outs = []
    for i in range(s // Q_BLOCK):
        q0 = i * Q_BLOCK
        end = q0 + Q_BLOCK
        qb = q[:, :, :, q0:end]
        kb = k[:, :, :, :end]
        vb = v[:, :, :end]
        scores = jnp.einsum('bhnqd,bhnkd->bhnqk', qb, kb).astype(jnp.float32) * scale
        q_chunk = (q0 + jnp.arange(Q_BLOCK)) // CHUNK
        k_chunk = jnp.arange(end) // CHUNK
        mask = k_chunk[None, :] <= q_chunk[:, None]
        probs = jax.nn.softmax(jnp.where(mask, scores, -jnp.inf), axis=-1)
        attn = probs[:, :, 0] - lam * probs[:, :, 1]
        outs.append(jnp.einsum('bhqk,bhkd->bhqd', attn.astype(v.dtype), vb))
    o = jnp.concatenate(outs, axis=2)
    o = rmsnorm(o, head_g[None, :, None, :]) * (1.0 - lam_init)
    return o.transpose(0, 2, 1, 3).reshape(b, s, ATTN_WIDTH)


def causal_dwconv(u, w, bias):
    s = u.shape[1]
    upad = jnp.pad(u, ((0, 0), (CONV_WIDTH - 1, 0), (0, 0)))
    out = bias
    for j in range(CONV_WIDTH):
        out = out + w[j] * upad[:, j:j + s]
    return out


def setup_inputs(seed: int = 0) -> dict:
    key = jax.random.key(seed)
    ks = jax.random.split(key, 20)
    nrm = lambda k, shape, s: jax.random.normal(k, shape, jnp.float32) * s
    x = jax.random.normal(ks[0], (BATCH, SEQ, D_MODEL), jnp.float32)
    offset = jax.random.randint(ks[1], (BATCH,), 0, 10000, dtype=jnp.int32)
    positions = offset[:, None] + jnp.arange(SEQ, dtype=jnp.int32)[None, :]
    return {
        "x": x,
        "positions": positions,
        "norm_mix_g": 1.0 + nrm(ks[2], (DEPTH, D_MODEL), 0.05),
        "w_in": nrm(ks[3], (DEPTH, D_MODEL, IN_WIDTH), D_MODEL ** -0.5),
        "pool_w": nrm(ks[4], (DEPTH, POOL_GROUPS, POOL_GROUP_DIM, POOL_GROUP_DIM), POOL_GROUP_DIM ** -0.5),
        "pool_scale": 1.0 + nrm(ks[5], (DEPTH, POOL_WIDTH), 0.05),
        "lambda_q1": nrm(ks[6], (DEPTH, DIFF_HEAD_DIM), 0.1),
        "lambda_k1": nrm(ks[7], (DEPTH, DIFF_HEAD_DIM), 0.1),
        "lambda_q2": nrm(ks[8], (DEPTH, DIFF_HEAD_DIM), 0.1),
        "lambda_k2": nrm(ks[9], (DEPTH, DIFF_HEAD_DIM), 0.1),
        "attn_norm_g": 1.0 + nrm(ks[10], (DEPTH, N_DIFF_HEADS, V_HEAD_DIM), 0.05),
        "w_o": nrm(ks[11], (DEPTH, MIX_WIDTH, D_MODEL), MIX_WIDTH ** -0.5),
        "norm_ffn_g": 1.0 + nrm(ks[12], (DEPTH, D_MODEL), 0.05),
        "w_up": nrm(ks[13], (DEPTH, D_MODEL, 2 * D_FF), D_MODEL ** -0.5),
        "conv_w": nrm(ks[14], (DEPTH, CONV_WIDTH, 2 * D_FF), CONV_WIDTH ** -0.5),
        "conv_b": nrm(ks[15], (DEPTH, 2 * D_FF), 0.02),
        "w_down": nrm(ks[16], (DEPTH, D_FF, D_MODEL), D_FF ** -0.5),
        "norm_final_g": 1.0 + nrm(ks[17], (D_MODEL,), 0.05),
    }


def reference(x, positions, norm_mix_g, w_in, pool_w, pool_scale, lambda_q1, lambda_k1,
              lambda_q2, lambda_k2, attn_norm_g, w_o, norm_ffn_g, w_up, conv_w, conv_b,
              w_down, norm_final_g):
    b, s, _ = x.shape
    cos, sin = rope_tables(positions, x.dtype)
    for l in range(DEPTH):
        h = rmsnorm(x, norm_mix_g[l])
        proj = h @ w_in[l]
        p, q, k, v = jnp.split(
            proj, [POOL_WIDTH, POOL_WIDTH + ATTN_WIDTH, POOL_WIDTH + 2 * ATTN_WIDTH], axis=-1)
        pool_out = multiscale_pool(p, pool_w[l], pool_scale[l])
        q = apply_rope(q.reshape(b, s, N_DIFF_HEADS, 2, DIFF_HEAD_DIM), cos, sin)
        k = apply_rope(k.reshape(b, s, N_DIFF_HEADS, 2, DIFF_HEAD_DIM), cos, sin)
        q = q.transpose(0, 2, 3, 1, 4)
        k = k.transpose(0, 2, 3, 1, 4)
        v = v.reshape(b, s, N_DIFF_HEADS, V_HEAD_DIM).transpose(0, 2, 1, 3)
        lam_init = 0.8 - 0.6 * float(np.exp(-0.3 * l))
        lam = (jnp.exp(jnp.sum(lambda_q1[l].astype(jnp.float32) * lambda_k1[l].astype(jnp.float32)))
               - jnp.exp(jnp.sum(lambda_q2[l].astype(jnp.float32) * lambda_k2[l].astype(jnp.float32)))
               + lam_init)
        attn_out = diff_attention(q, k, v, lam, lam_init, attn_norm_g[l])
        mix = jnp.concatenate([pool_out, attn_out], axis=-1) @ w_o[l]
        x = x + mix
        h = rmsnorm(x, norm_ffn_g[l])
        u = causal_dwconv(h @ w_up[l], conv_w[l], conv_b[l])
        gate, val = jnp.split(u, 2, axis=-1)
        x = x + (jax.nn.silu(gate) * val) @ w_down[l]
    return rmsnorm(x, norm_final_g)
```

```python
import functools

import numpy as np
import jax
import jax.numpy as jnp
from jax import lax
from jax.experimental import pallas as pl
from jax.experimental.pallas import tpu as pltpu

CHUNK = 64
POOL_WINDOWS = (2, 4, 8, 16)
POOL_GROUP_DIM = 128
DIFF_HEAD_DIM = 64
V_HEAD_DIM = 2 * DIFF_HEAD_DIM
CONV_WIDTH = 3
ROPE_THETA = 10000.0
EPS = 1e-6

LANES = 128
SUBLANES = 8
MXU_DIM = 256

POOL_HALO = 16
CONV_HALO = SUBLANES

BF16 = jnp.bfloat16
F32 = jnp.float32


def _dot(a, b):
    return jnp.dot(a, b, preferred_element_type=F32)


def _shift_rows(a, k):
    return pltpu.roll(a, k, axis=0)


def _inproj_kernel(x_ref, pos_ref, invf_ref, g_ref, w_ref, pw_ref, ps_ref,
                   pool_ref, q_ref, k_ref, v_ref, carry_ref, *, tm, tiles_per_seq,
                   pool_width, attn_width):
    seq_tile = pl.program_id(0) % tiles_per_seq

    @pl.when(seq_tile == 0)
    def _():
        carry_ref[...] = jnp.zeros_like(carry_ref)

    x = x_ref[...]
    ms = jnp.mean(x * x, axis=-1, keepdims=True)
    h = ((x * lax.rsqrt(ms + EPS)) * g_ref[...]).astype(BF16)

    p = _dot(h, w_ref[:, 0:pool_width])
    t = seq_tile * tm + lax.broadcasted_iota(jnp.int32, (tm, POOL_GROUP_DIM), 0)
    pooled_out = []
    for g, w in enumerate(POOL_WINDOWS):
        cols = slice(g * POOL_GROUP_DIM, (g + 1) * POOL_GROUP_DIM)
        pg = p[:, cols]
        s = jnp.concatenate([carry_ref[:, cols], pg], axis=0)
        span = 1
        while span < w:
            s = s + _shift_rows(s, span)
            span *= 2
        count = jnp.minimum(t + 1, w).astype(F32)
        d = (s[POOL_HALO:] / count - pg).astype(BF16)
        pooled_out.append(_dot(d, pw_ref[g]))
    carry_ref[...] = p[tm - POOL_HALO:, :]
    pool_ref[...] = (jnp.concatenate(pooled_out, axis=1) * ps_ref[...]).astype(pool_ref.dtype)

    ang = pos_ref[...] * invf_ref[...]
    cos = jnp.cos(ang)
    sin = jnp.sin(ang)
    lane = lax.broadcasted_iota(jnp.int32, (tm, LANES), 1)
    first_half = (lane % DIFF_HEAD_DIM) < (DIFF_HEAD_DIM // 2)

    def rope(tt, scale):
        outs = []
        for j in range(attn_width // LANES):
            tj = tt[:, j * LANES:(j + 1) * LANES]
            partner = jnp.where(first_half,
                                pltpu.roll(tj, LANES - DIFF_HEAD_DIM // 2, axis=1),
                                pltpu.roll(tj, DIFF_HEAD_DIM // 2, axis=1))
            outs.append((tj * cos + partner * sin) * scale)
        return jnp.concatenate(outs, axis=1)

    q0 = pool_width
    k0 = pool_width + attn_width
    v0 = pool_width + 2 * attn_width
    q = _dot(h, w_ref[:, q0:k0])
    q_ref[...] = rope(q, DIFF_HEAD_DIM ** -0.5).astype(q_ref.dtype)
    k = _dot(h, w_ref[:, k0:v0])
    k_ref[...] = rope(k, 1.0).astype(k_ref.dtype)
    v_ref[...] = _dot(h, w_ref[:, v0:v0 + attn_width]).astype(v_ref.dtype)


def _inproj(x2, pos_b, invf, g, w, pool_w, pool_scale, *, seq, tm):
    T, d_model = x2.shape
    pool_width = pool_scale.shape[-1]
    attn_width = (w.shape[1] - pool_width) // 3
    n_groups = pool_w.shape[0]
    kern = functools.partial(_inproj_kernel, tm=tm, tiles_per_seq=seq // tm,
                             pool_width=pool_width, attn_width=attn_width)
    row = lambda i: (i, 0)
    const2 = lambda i: (0, 0)
    out_sds = lambda width: jax.ShapeDtypeStruct((T, width), BF16)
    return pl.pallas_call(
        kern,
        grid=(T // tm,),
        in_specs=[
            pl.BlockSpec((tm, d_model), row),
            pl.BlockSpec((tm, LANES), row),
            pl.BlockSpec((1, LANES), const2),
            pl.BlockSpec((1, d_model), const2),
            pl.BlockSpec(w.shape, const2),
            pl.BlockSpec(pool_w.shape, lambda i: (0, 0, 0)),
            pl.BlockSpec((1, pool_width), const2),
        ],
        out_specs=[
            pl.BlockSpec((tm, pool_width), row),
            pl.BlockSpec((tm, attn_width), row),
            pl.BlockSpec((tm, attn_width), row),
            pl.BlockSpec((tm, attn_width), row),
        ],
        out_shape=[out_sds(pool_width), out_sds(attn_width), out_sds(attn_width), out_sds(attn_width)],
        scratch_shapes=[pltpu.VMEM((POOL_HALO, n_groups * POOL_GROUP_DIM), F32)],
        compiler_params=pltpu.CompilerParams(
            dimension_semantics=("arbitrary",),
            vmem_limit_bytes=48 * 1024 * 1024),
        name="inproj_pool_rope",
    )(x2, pos_b, invf, g, w, pool_w, pool_scale)


def _attn_kernel(lam_ref, g_ref, q_ref, k_ref, v_ref, o_ref, m_ref, l_ref, acc_ref,
                 *, tq, tk, lam_init):
    i = pl.program_id(2)
    rows = 2 * tq

    q = q_ref[...]
    lane = lax.broadcasted_iota(jnp.int32, q.shape, 1)
    zero = jnp.zeros_like(q)
    qq = jnp.concatenate([jnp.where(lane < DIFF_HEAD_DIM, q, zero),
                          jnp.where(lane >= DIFF_HEAD_DIM, q, zero)], axis=0)

    m_ref[...] = jnp.full(m_ref.shape, -jnp.inf, F32)
    l_ref[...] = jnp.zeros(l_ref.shape, F32)
    acc_ref[...] = jnp.zeros(acc_ref.shape, F32)

    def step(j, masked):
        off = pl.multiple_of(j * tk, tk)
        kt = k_ref[pl.ds(off, tk), :]
        vt = v_ref[pl.ds(off, tk), :]
        s = lax.dot_general(qq, kt, (((1,), (1,)), ((), ())), preferred_element_type=F32)
        if masked:
            r = lax.broadcasted_iota(jnp.int32, (rows, tk), 0)
            q_chunk = (i * tq + jnp.where(r >= tq, r - tq, r)) >> 6
            k_chunk = (off + lax.broadcasted_iota(jnp.int32, (rows, tk), 1)) >> 6
            s = jnp.where(k_chunk <= q_chunk, s, -jnp.inf)
        m_prev = m_ref[...]
        m_new = jnp.maximum(m_prev, jnp.max(s, axis=1, keepdims=True))
        alpha = jnp.exp(m_prev - m_new)
        l_part = alpha * l_ref[...]
        ps = []
        for c in range(tk // LANES):
            pc = jnp.exp(s[:, c * LANES:(c + 1) * LANES] - m_new)
            l_part = l_part + pc
            ps.append(pc.astype(BF16))
        l_ref[...] = l_part
        acc_ref[...] = alpha * acc_ref[...] + _dot(jnp.concatenate(ps, axis=1), vt)
        m_ref[...] = m_new

    n_full = (i * tq) // tk

    def body(j, carry):
        step(j, masked=False)
        return carry

    lax.fori_loop(0, n_full, body, 0)
    step(n_full, masked=True)

    lam_vec = lam_ref[...]
    lam = (jnp.exp(jnp.sum(lam_vec[0:1] * lam_vec[1:2], axis=1, keepdims=True))
           - jnp.exp(jnp.sum(lam_vec[2:3] * lam_vec[3:4], axis=1, keepdims=True))
           + lam_init)
    o = acc_ref[...] / jnp.sum(l_ref[...], axis=1, keepdims=True)
    od = o[:tq] - lam * o[tq:]
    y = od * lax.rsqrt(jnp.mean(od * od, axis=-1, keepdims=True) + EPS)
    o_ref[...] = ((y * g_ref[...]) * (1.0 - lam_init)).astype(o_ref.dtype)


def _diff_attention(q, k, v, lam_vec, head_g, *, batch, seq, lam_init, tq, tk):
    T, attn_width = q.shape
    n_heads = attn_width // V_HEAD_DIM
    nq = seq // tq
    kern = functools.partial(_attn_kernel, tq=tq, tk=tk, lam_init=lam_init)
    kv_spec = pl.BlockSpec((seq, V_HEAD_DIM), lambda b, h, i: (b, h))
    q_spec = pl.BlockSpec((tq, V_HEAD_DIM), lambda b, h, i: (b * nq + i, h))
    return pl.pallas_call(
        kern,
        grid=(batch, n_heads, nq),
        in_specs=[
            pl.BlockSpec(lam_vec.shape, lambda b, h, i: (0, 0)),
            pl.BlockSpec((None, 1, V_HEAD_DIM), lambda b, h, i: (h, 0, 0)),
            q_spec, kv_spec, kv_spec,
        ],
        out_specs=q_spec,
        out_shape=jax.ShapeDtypeStruct((T, attn_width), BF16),
        scratch_shapes=[
            pltpu.VMEM((2 * tq, LANES), F32),
            pltpu.VMEM((2 * tq, LANES), F32),
            pltpu.VMEM((2 * tq, V_HEAD_DIM), F32),
        ],
        compiler_params=pltpu.CompilerParams(
            dimension_semantics=("arbitrary", "arbitrary", "arbitrary"),
            vmem_limit_bytes=32 * 1024 * 1024),
        name="diff_attention",
    )(lam_vec, head_g, q, k, v)


def _ffn_kernel(x_ref, pool_ref, attn_ref, wo_ref, gf_ref, wup_ref, cw_ref, cb_ref, wd_ref,
                gl_ref, o_ref, carry_ref, x1_ref, h_ref, acc_ref,
                *, tm, tiles_per_seq, n_chunks, apply_final):
    seq_tile = pl.program_id(0) % tiles_per_seq

    @pl.when(seq_tile == 0)
    def _():
        carry_ref[...] = jnp.zeros_like(carry_ref)

    mixed = jnp.concatenate([pool_ref[...], attn_ref[...]], axis=1)
    x1 = x_ref[...] + _dot(mixed, wo_ref[...])
    x1_ref[...] = x1
    ms = jnp.mean(x1 * x1, axis=-1, keepdims=True)
    h_ref[...] = ((x1 * lax.rsqrt(ms + EPS)) * gf_ref[...]).astype(BF16)
    acc_ref[...] = jnp.zeros_like(acc_ref)

    def conv_branch(h, idx):
        u = _dot(h, wup_ref[idx])
        ext = jnp.concatenate([carry_ref[idx], u], axis=0)
        u1 = _shift_rows(ext, 1)[CONV_HALO:]
        u2 = _shift_rows(ext, 2)[CONV_HALO:]
        carry_ref[idx] = u[tm - CONV_HALO:, :]
        cw = cw_ref[idx]
        return ((cb_ref[idx] + cw[0:1] * u2) + cw[1:2] * u1) + cw[2:3] * u

    def chunk_body(c, carry):
        h = h_ref[...]
        gate = conv_branch(h, c)
        val = conv_branch(h, n_chunks + c)
        act = (gate * (1.0 / (1.0 + jnp.exp(-gate)))) * val
        acc_ref[...] += _dot(act.astype(BF16), wd_ref[c])
        return carry

    lax.fori_loop(0, n_chunks, chunk_body, 0)

    x2 = x1_ref[...] + acc_ref[...]
    if apply_final:
        ms2 = jnp.mean(x2 * x2, axis=-1, keepdims=True)
        x2 = (x2 * lax.rsqrt(ms2 + EPS)) * gl_ref[...]
    o_ref[...] = x2


def _ffn(x2d, pool_out, attn_out, wo, g_ffn, wup3, cw3, cb3, wd3, g_final,
         *, seq, tm, apply_final):
    T, d_model = x2d.shape
    n_chunks = wd3.shape[0]
    chunk = wd3.shape[1]
    kern = functools.partial(_ffn_kernel, tm=tm, tiles_per_seq=seq // tm,
                             n_chunks=n_chunks, apply_final=apply_final)
    row = lambda i: (i, 0)
    const2 = lambda i: (0, 0)
    const3 = lambda i: (0, 0, 0)
    resident = dict(pipeline_mode=pl.Buffered(1))
    return pl.pallas_call(
        kern,
        grid=(T // tm,),
        in_specs=[
            pl.BlockSpec((tm, d_model), row),
            pl.BlockSpec((tm, pool_out.shape[1]), row),
            pl.BlockSpec((tm, attn_out.shape[1]), row),
            pl.BlockSpec(wo.shape, const2, **resident),
            pl.BlockSpec((1, d_model), const2),
            pl.BlockSpec(wup3.shape, const3, **resident),
            pl.BlockSpec(cw3.shape, const3),
            pl.BlockSpec(cb3.shape, const3),
            pl.BlockSpec(wd3.shape, const3, **resident),
            pl.BlockSpec((1, d_model), const2),
        ],
        out_specs=pl.BlockSpec((tm, d_model), row),
        out_shape=jax.ShapeDtypeStruct((T, d_model), F32),
        scratch_shapes=[
            pltpu.VMEM((2 * n_chunks, CONV_HALO, chunk), F32),
            pltpu.VMEM((tm, d_model), F32),
            pltpu.VMEM((tm, d_model), BF16),
            pltpu.VMEM((tm, d_model), F32),
        ],
        compiler_params=pltpu.CompilerParams(
            dimension_semantics=("arbitrary",),
            vmem_limit_bytes=56 * 1024 * 1024),
        name="outproj_convffn",
    )(x2d, pool_out, attn_out, wo, g_ffn, wup3, cw3, cb3, wd3, g_final)


def _chunk_columns(w, chunk):
    rows, cols = w.shape
    return w.reshape(rows, cols // chunk, chunk).transpose(1, 0, 2)


def kernel(x, positions, norm_mix_g, w_in, pool_w, pool_scale, lambda_q1, lambda_k1, lambda_q2,
           lambda_k2, attn_norm_g, w_o, norm_ffn_g, w_up, conv_w, conv_b, w_down, norm_final_g):
    batch, seq, d_model = x.shape
    depth = w_in.shape[0]
    T = batch * seq
    d_ff = w_down.shape[1]
    n_heads = attn_norm_g.shape[1]
    chunk = MXU_DIM
    n_chunks = d_ff // chunk
    assert d_ff % chunk == 0 and seq % 512 == 0

    x2d = x.reshape(T, d_model)
    pos_b = jnp.broadcast_to(positions.astype(F32).reshape(T, 1), (T, LANES))
    inv_freq = ROPE_THETA ** (-jnp.arange(0, DIFF_HEAD_DIM, 2, dtype=F32) / DIFF_HEAD_DIM)
    invf = jnp.concatenate([-inv_freq, inv_freq, -inv_freq, inv_freq]).reshape(1, LANES)

    for l in range(depth):
        lam_init = 0.8 - 0.6 * float(np.exp(-0.3 * l))
        pool_out, q, k, v = _inproj(
            x2d, pos_b, invf, norm_mix_g[l].reshape(1, d_model), w_in[l].astype(BF16),
            pool_w[l].astype(BF16), pool_scale[l].reshape(1, -1), seq=seq, tm=512)
        lam_vec = jnp.stack([lambda_q1[l], lambda_k1[l], lambda_q2[l], lambda_k2[l]]).astype(F32)
        attn_out = _diff_attention(
            q, k, v, lam_vec, attn_norm_g[l].reshape(n_heads, 1, V_HEAD_DIM),
            batch=batch, seq=seq, lam_init=lam_init, tq=256, tk=512)
        x2d = _ffn(
            x2d, pool_out, attn_out, w_o[l].astype(BF16), norm_ffn_g[l].reshape(1, d_model),
            _chunk_columns(w_up[l].astype(BF16), chunk),
            _chunk_columns(conv_w[l], chunk),
            _chunk_columns(conv_b[l].reshape(1, -1), chunk),
            w_down[l].astype(BF16).reshape(n_chunks, chunk, d_model),
            norm_final_g.reshape(1, d_model),
            seq=seq, tm=512, apply_final=(l == depth - 1))
    return x2d.reshape(batch, seq, d_model)
```

```python
import functools

import numpy as np
import jax
import jax.numpy as jnp
from jax import lax
from jax.experimental import pallas as pl
from jax.experimental.pallas import tpu as pltpu

CHUNK = 64
POOL_WINDOWS = (2, 4, 8, 16)
POOL_GROUP_DIM = 128
DIFF_HEAD_DIM = 64
V_HEAD_DIM = 2 * DIFF_HEAD_DIM
CONV_WIDTH = 3
ROPE_THETA = 10000.0
EPS = 1e-6
LOG2_E = 1.4426950408889634

LANES = 128
SUBLANES = 8
MXU_DIM = 256

POOL_HALO = 16
CONV_HALO = SUBLANES

BF16 = jnp.bfloat16
F32 = jnp.float32


def _dot(a, b):
    return jnp.dot(a, b, preferred_element_type=F32)


def _shift_rows(a, k):
    return pltpu.roll(a, k, axis=0)


def _inproj_kernel(x_ref, pos_ref, invf_ref, g_ref, w_ref, pw_ref, ps_ref,
                   pool_ref, q_ref, k_ref, v_ref, carry_ref, *, tm, tiles_per_seq,
                   pool_width, attn_width):
    seq_tile = pl.program_id(0) % tiles_per_seq

    @pl.when(seq_tile == 0)
    def _():
        carry_ref[...] = jnp.zeros_like(carry_ref)

    x = x_ref[...]
    ms = jnp.mean(x * x, axis=-1, keepdims=True)
    h = ((x * lax.rsqrt(ms + EPS)) * g_ref[...]).astype(BF16)

    p = _dot(h, w_ref[:, 0:pool_width])
    t = seq_tile * tm + lax.broadcasted_iota(jnp.int32, (tm, POOL_GROUP_DIM), 0)
    pooled_out = []
    for g, w in enumerate(POOL_WINDOWS):
        cols = slice(g * POOL_GROUP_DIM, (g + 1) * POOL_GROUP_DIM)
        pg = p[:, cols]
        s = jnp.concatenate([carry_ref[:, cols], pg], axis=0)
        span = 1
        while span < w:
            s = s + _shift_rows(s, span)
            span *= 2
        count = jnp.minimum(t + 1, w).astype(F32)
        d = (s[POOL_HALO:] / count - pg).astype(BF16)
        pooled_out.append(_dot(d, pw_ref[g]))
    carry_ref[...] = p[tm - POOL_HALO:, :]
    pool_ref[...] = (jnp.concatenate(pooled_out, axis=1) * ps_ref[...]).astype(pool_ref.dtype)

    ang = pos_ref[...] * invf_ref[...]
    cos = jnp.cos(ang)
    sin = jnp.sin(ang)
    lane = lax.broadcasted_iota(jnp.int32, (tm, LANES), 1)
    first_half = (lane % DIFF_HEAD_DIM) < (DIFF_HEAD_DIM // 2)

    def rope(tt, scale):
        outs = []
        for j in range(attn_width // LANES):
            tj = tt[:, j * LANES:(j + 1) * LANES]
            partner = jnp.where(first_half,
                                pltpu.roll(tj, LANES - DIFF_HEAD_DIM // 2, axis=1),
                                pltpu.roll(tj, DIFF_HEAD_DIM // 2, axis=1))
            outs.append((tj * cos + partner * sin) * scale)
        return jnp.concatenate(outs, axis=1)

    q0 = pool_width
    k0 = pool_width + attn_width
    v0 = pool_width + 2 * attn_width
    q = _dot(h, w_ref[:, q0:k0])
    q_ref[...] = rope(q, DIFF_HEAD_DIM ** -0.5 * LOG2_E).astype(q_ref.dtype)
    k = _dot(h, w_ref[:, k0:v0])
    k_ref[...] = rope(k, 1.0).astype(k_ref.dtype)
    v_ref[...] = _dot(h, w_ref[:, v0:v0 + attn_width]).astype(v_ref.dtype)


def _inproj(x2, pos_b, invf, g, w, pool_w, pool_scale, *, seq, tm):
    T, d_model = x2.shape
    pool_width = pool_scale.shape[-1]
    attn_width = (w.shape[1] - pool_width) // 3
    n_groups = pool_w.shape[0]
    kern = functools.partial(_inproj_kernel, tm=tm, tiles_per_seq=seq // tm,
                             pool_width=pool_width, attn_width=attn_width)
    row = lambda i: (i, 0)
    const2 = lambda i: (0, 0)
    out_sds = lambda width: jax.ShapeDtypeStruct((T, width), BF16)
    return pl.pallas_call(
        kern,
        grid=(T // tm,),
        in_specs=[
            pl.BlockSpec((tm, d_model), row),
            pl.BlockSpec((tm, LANES), row),
            pl.BlockSpec((1, LANES), const2),
            pl.BlockSpec((1, d_model), const2),
            pl.BlockSpec(w.shape, const2),
            pl.BlockSpec(pool_w.shape, lambda i: (0, 0, 0)),
            pl.BlockSpec((1, pool_width), const2),
        ],
        out_specs=[
            pl.BlockSpec((tm, pool_width), row),
            pl.BlockSpec((tm, attn_width), row),
            pl.BlockSpec((tm, attn_width), row),
            pl.BlockSpec((tm, attn_width), row),
        ],
        out_shape=[out_sds(pool_width), out_sds(attn_width), out_sds(attn_width), out_sds(attn_width)],
        scratch_shapes=[pltpu.VMEM((POOL_HALO, n_groups * POOL_GROUP_DIM), F32)],
        compiler_params=pltpu.CompilerParams(
            dimension_semantics=("arbitrary",),
            vmem_limit_bytes=48 * 1024 * 1024),
        name="inproj_pool_rope",
    )(x2, pos_b, invf, g, w, pool_w, pool_scale)


def _attn_kernel(lam_ref, g_ref, q_ref, k_ref, v_ref, o_ref, m_ref, l_ref, acc_ref,
                 *, tq, tk, heads, lam_init):
    i = pl.program_id(2)
    rows = 2 * tq

    lane = lax.broadcasted_iota(jnp.int32, (tq, V_HEAD_DIM), 1)
    qqs = []
    for hh in range(heads):
        q = q_ref[:, hh * V_HEAD_DIM:(hh + 1) * V_HEAD_DIM]
        zero = jnp.zeros_like(q)
        qqs.append(jnp.concatenate([jnp.where(lane < DIFF_HEAD_DIM, q, zero),
                                    jnp.where(lane >= DIFF_HEAD_DIM, q, zero)], axis=0))

    m_ref[...] = jnp.full(m_ref.shape, -jnp.inf, F32)
    l_ref[...] = jnp.zeros(l_ref.shape, F32)
    acc_ref[...] = jnp.zeros(acc_ref.shape, F32)

    def head_step(hh, off, masked):
        cols = slice(hh * V_HEAD_DIM, (hh + 1) * V_HEAD_DIM)
        kt = k_ref[pl.ds(off, tk), cols]
        vt = v_ref[pl.ds(off, tk), cols]
        s = lax.dot_general(qqs[hh], kt, (((1,), (1,)), ((), ())),
                            preferred_element_type=F32)
        if masked:
            r = lax.broadcasted_iota(jnp.int32, (rows, tk), 0)
            q_chunk = (i * tq + jnp.where(r >= tq, r - tq, r)) >> 6
            k_chunk = (off + lax.broadcasted_iota(jnp.int32, (rows, tk), 1)) >> 6
            s = jnp.where(k_chunk <= q_chunk, s, -jnp.inf)
        m_prev = m_ref[hh]
        m_new = jnp.maximum(m_prev, jnp.max(s, axis=1, keepdims=True))
        alpha = jnp.exp2(m_prev - m_new)
        l_part = alpha * l_ref[hh]
        ps = []
        for c in range(tk // LANES):
            pc = jnp.exp2(s[:, c * LANES:(c + 1) * LANES] - m_new)
            l_part = l_part + pc
            ps.append(pc.astype(BF16))
        l_ref[hh] = l_part
        acc_ref[hh] = alpha * acc_ref[hh] + _dot(jnp.concatenate(ps, axis=1), vt)
        m_ref[hh] = m_new

    def step(j, masked):
        off = pl.multiple_of(j * tk, tk)
        for hh in range(heads):
            head_step(hh, off, masked)

    n_full = (i * tq) // tk

    def body(j, carry):
        step(j, masked=False)
        return carry

    lax.fori_loop(0, n_full, body, 0)
    step(n_full, masked=True)

    lam_vec = lam_ref[...]
    lam = (jnp.exp(jnp.sum(lam_vec[0:1] * lam_vec[1:2], axis=1, keepdims=True))
           - jnp.exp(jnp.sum(lam_vec[2:3] * lam_vec[3:4], axis=1, keepdims=True))
           + lam_init)
    for hh in range(heads):
        o = acc_ref[hh] / jnp.sum(l_ref[hh], axis=1, keepdims=True)
        od = o[:tq] - lam * o[tq:]
        y = od * lax.rsqrt(jnp.mean(od * od, axis=-1, keepdims=True) + EPS)
        o_ref[:, hh * V_HEAD_DIM:(hh + 1) * V_HEAD_DIM] = (
            (y * g_ref[hh]) * (1.0 - lam_init)).astype(o_ref.dtype)


def _diff_attention(q, k, v, lam_vec, head_g, *, batch, seq, lam_init, tq, tk, heads):
    T, attn_width = q.shape
    n_heads = attn_width // V_HEAD_DIM
    nq = seq // tq
    width = heads * V_HEAD_DIM
    kern = functools.partial(_attn_kernel, tq=tq, tk=tk, heads=heads, lam_init=lam_init)
    kv_spec = pl.BlockSpec((seq, width), lambda b, h, i: (b, h))
    q_spec = pl.BlockSpec((tq, width), lambda b, h, i: (b * nq + i, h))
    return pl.pallas_call(
        kern,
        grid=(batch, n_heads // heads, nq),
        in_specs=[
            pl.BlockSpec(lam_vec.shape, lambda b, h, i: (0, 0)),
            pl.BlockSpec((heads, 1, V_HEAD_DIM), lambda b, h, i: (h, 0, 0)),
            q_spec, kv_spec, kv_spec,
        ],
        out_specs=q_spec,
        out_shape=jax.ShapeDtypeStruct((T, attn_width), BF16),
        scratch_shapes=[
            pltpu.VMEM((heads, 2 * tq, LANES), F32),
            pltpu.VMEM((heads, 2 * tq, LANES), F32),
            pltpu.VMEM((heads, 2 * tq, V_HEAD_DIM), F32),
        ],
        compiler_params=pltpu.CompilerParams(
            dimension_semantics=("arbitrary", "arbitrary", "arbitrary"),
            vmem_limit_bytes=32 * 1024 * 1024),
        name="diff_attention",
    )(lam_vec, head_g, q, k, v)


def _ffn_kernel(x_ref, pool_ref, attn_ref, wo_ref, gf_ref, wup_ref, cw_ref, cb_ref, wd_ref,
                gl_ref, o_ref, carry_ref, x1_ref, h_ref, acc_ref, u00_ref, u01_ref, u10_ref, u11_ref,
                *, tm, slab, tiles_per_seq, n_chunks, apply_final):
    u_refs = ((u00_ref, u01_ref), (u10_ref, u11_ref))
    seq_tile = pl.program_id(0) % tiles_per_seq

    @pl.when(seq_tile == 0)
    def _():
        carry_ref[...] = jnp.zeros_like(carry_ref)

    mixed = jnp.concatenate([pool_ref[...], attn_ref[...]], axis=1)
    x1 = x_ref[...] + _dot(mixed, wo_ref[...])
    x1_ref[...] = x1
    ms = jnp.mean(x1 * x1, axis=-1, keepdims=True)
    h_ref[...] = ((x1 * lax.rsqrt(ms + EPS)) * gf_ref[...]).astype(BF16)
    acc_ref[...] = jnp.zeros_like(acc_ref)

    n_slabs = tm // slab

    def up_proj(c, slot, r):
        h = h_ref[r * slab:(r + 1) * slab, :]
        for br, idx in enumerate((c, n_chunks + c)):
            u_ref = u_refs[slot][br]
            if r == 0:
                u_ref[0:CONV_HALO, :] = carry_ref[idx]
            u_ref[CONV_HALO + r * slab:CONV_HALO + (r + 1) * slab, :] = _dot(h, wup_ref[idx])
            if r == n_slabs - 1:
                carry_ref[idx] = u_ref[tm:, :]

    def conv_branch(slot, br, idx, r):
        r0 = CONV_HALO + r * slab
        u_ref = u_refs[slot][br]
        u = u_ref[r0:r0 + slab, :]
        u1 = u_ref[r0 - 1:r0 - 1 + slab, :]
        u2 = u_ref[r0 - 2:r0 - 2 + slab, :]
        cw = cw_ref[idx]
        return ((cb_ref[idx] + cw[0:1] * u2) + cw[1:2] * u1) + cw[2:3] * u

    for r in range(n_slabs):
        up_proj(0, 0, r)
    for c in range(n_chunks):
        slot = c % 2
        for r in range(n_slabs):
            if c + 1 < n_chunks:
                up_proj(c + 1, 1 - slot, r)
            gate = conv_branch(slot, 0, c, r)
            val = conv_branch(slot, 1, n_chunks + c, r)
            act = (gate * (1.0 / (1.0 + jnp.exp(-gate)))) * val
            acc_ref[r * slab:(r + 1) * slab, :] += _dot(act.astype(BF16), wd_ref[c])

    x2 = x1_ref[...] + acc_ref[...]
    if apply_final:
        ms2 = jnp.mean(x2 * x2, axis=-1, keepdims=True)
        x2 = (x2 * lax.rsqrt(ms2 + EPS)) * gl_ref[...]
    o_ref[...] = x2


def _ffn(x2d, pool_out, attn_out, wo, g_ffn, wup3, cw3, cb3, wd3, g_final,
         *, seq, tm, apply_final):
    T, d_model = x2d.shape
    n_chunks = wd3.shape[0]
    chunk = wd3.shape[1]
    kern = functools.partial(_ffn_kernel, tm=tm, tiles_per_seq=seq // tm,
                             slab=MXU_DIM, n_chunks=n_chunks, apply_final=apply_final)
    row = lambda i: (i, 0)
    const2 = lambda i: (0, 0)
    const3 = lambda i: (0, 0, 0)
    resident = dict(pipeline_mode=pl.Buffered(1))
    return pl.pallas_call(
        kern,
        grid=(T // tm,),
        in_specs=[
            pl.BlockSpec((tm, d_model), row),
            pl.BlockSpec((tm, pool_out.shape[1]), row),
            pl.BlockSpec((tm, attn_out.shape[1]), row),
            pl.BlockSpec(wo.shape, const2, **resident),
            pl.BlockSpec((1, d_model), const2),
            pl.BlockSpec(wup3.shape, const3, **resident),
            pl.BlockSpec(cw3.shape, const3),
            pl.BlockSpec(cb3.shape, const3),
            pl.BlockSpec(wd3.shape, const3, **resident),
            pl.BlockSpec((1, d_model), const2),
        ],
        out_specs=pl.BlockSpec((tm, d_model), row),
        out_shape=jax.ShapeDtypeStruct((T, d_model), F32),
        scratch_shapes=[
            pltpu.VMEM((2 * n_chunks, CONV_HALO, chunk), F32),
            pltpu.VMEM((tm, d_model), F32),
            pltpu.VMEM((tm, d_model), BF16),
            pltpu.VMEM((tm, d_model), F32),
        ] + [pltpu.VMEM((CONV_HALO + tm, chunk), F32)] * 4,
        compiler_params=pltpu.CompilerParams(
            dimension_semantics=("arbitrary",),
            vmem_limit_bytes=56 * 1024 * 1024),
        name="outproj_convffn",
    )(x2d, pool_out, attn_out, wo, g_ffn, wup3, cw3, cb3, wd3, g_final)


def _chunk_columns(w, chunk):
    rows, cols = w.shape
    return w.reshape(rows, cols // chunk, chunk).transpose(1, 0, 2)


def kernel(x, positions, norm_mix_g, w_in, pool_w, pool_scale, lambda_q1, lambda_k1, lambda_q2,
           lambda_k2, attn_norm_g, w_o, norm_ffn_g, w_up, conv_w, conv_b, w_down, norm_final_g):
    batch, seq, d_model = x.shape
    depth = w_in.shape[0]
    T = batch * seq
    d_ff = w_down.shape[1]
    n_heads = attn_norm_g.shape[1]
    chunk = MXU_DIM
    n_chunks = d_ff // chunk
    assert d_ff % chunk == 0 and seq % 512 == 0

    x2d = x.reshape(T, d_model)
    pos_b = jnp.broadcast_to(positions.astype(F32).reshape(T, 1), (T, LANES))
    inv_freq = ROPE_THETA ** (-jnp.arange(0, DIFF_HEAD_DIM, 2, dtype=F32) / DIFF_HEAD_DIM)
    invf = jnp.concatenate([-inv_freq, inv_freq, -inv_freq, inv_freq]).reshape(1, LANES)

    for l in range(depth):
        lam_init = 0.8 - 0.6 * float(np.exp(-0.3 * l))
        pool_out, q, k, v = _inproj(
            x2d, pos_b, invf, norm_mix_g[l].reshape(1, d_model), w_in[l].astype(BF16),
            pool_w[l].astype(BF16), pool_scale[l].reshape(1, -1), seq=seq, tm=512)
        lam_vec = jnp.stack([lambda_q1[l], lambda_k1[l], lambda_q2[l], lambda_k2[l]]).astype(F32)
        attn_out = _diff_attention(
            q, k, v, lam_vec, attn_norm_g[l].reshape(n_heads, 1, V_HEAD_DIM),
            batch=batch, seq=seq, lam_init=lam_init, tq=256, tk=512, heads=2)
        x2d = _ffn(
            x2d, pool_out, attn_out, w_o[l].astype(BF16), norm_ffn_g[l].reshape(1, d_model),
            _chunk_columns(w_up[l].astype(BF16), chunk),
            _chunk_columns(conv_w[l], chunk),
            _chunk_columns(conv_b[l].reshape(1, -1), chunk),
            w_down[l].astype(BF16).reshape(n_chunks, chunk, d_model),
            norm_final_g.reshape(1, d_model),
            seq=seq, tm=512, apply_final=(l == depth - 1))
    return x2d.reshape(batch, seq, d_model)
```

```python
import functools

import numpy as np
import jax
import jax.numpy as jnp
from jax import lax
from jax.experimental import pallas as pl
from jax.experimental.pallas import tpu as pltpu

CHUNK = 64
CHUNK_LOG2 = CHUNK.bit_length() - 1
POOL_WINDOWS = (2, 4, 8, 16)
POOL_GROUP_DIM = 128
DIFF_HEAD_DIM = 64
V_HEAD_DIM = 2 * DIFF_HEAD_DIM
CONV_WIDTH = 3
ROPE_THETA = 10000.0
EPS = 1e-6
LOG2_E = 1.4426950408889634

LANES = 128
SUBLANES = 8
MXU_DIM = 256

POOL_HALO = 16
CONV_HALO = SUBLANES
ONES_ROWS = 16
ACC_ROWS = V_HEAD_DIM + ONES_ROWS
MAX_WAYS = 8

BF16 = jnp.bfloat16
F32 = jnp.float32


def _dot(a, b):
    return jnp.dot(a, b, preferred_element_type=F32)


def _shift_rows(a, k):
    return pltpu.roll(a, k, axis=0)


def _inproj_kernel(x_ref, pos_ref, invf_ref, g_ref, w_ref, pw_ref, ps_ref,
                   pool_ref, q_ref, k_ref, v_ref, carry_ref, *, tm, tiles_per_seq,
                   pool_width, attn_width):
    seq_tile = pl.program_id(0) % tiles_per_seq

    @pl.when(seq_tile == 0)
    def _():
        carry_ref[...] = jnp.zeros_like(carry_ref)

    x = x_ref[...]
    ms = jnp.mean(x * x, axis=-1, keepdims=True)
    h = ((x * lax.rsqrt(ms + EPS)) * g_ref[...]).astype(BF16)

    p = _dot(h, w_ref[:, 0:pool_width])
    t = seq_tile * tm + lax.broadcasted_iota(jnp.int32, (tm, POOL_GROUP_DIM), 0)
    pooled_out = []
    for g, w in enumerate(POOL_WINDOWS):
        cols = slice(g * POOL_GROUP_DIM, (g + 1) * POOL_GROUP_DIM)
        pg = p[:, cols]
        s = jnp.concatenate([carry_ref[:, cols], pg], axis=0)
        span = 1
        while span < w:
            s = s + _shift_rows(s, span)
            span *= 2
        count = jnp.minimum(t + 1, w).astype(F32)
        d = (s[POOL_HALO:] / count - pg).astype(BF16)
        pooled_out.append(_dot(d, pw_ref[g]))
    carry_ref[...] = p[tm - POOL_HALO:, :]
    pool_ref[...] = (jnp.concatenate(pooled_out, axis=1) * ps_ref[...]).astype(pool_ref.dtype)

    ang = pos_ref[...] * invf_ref[...]
    cos = jnp.cos(ang)
    sin = jnp.sin(ang)
    lane = lax.broadcasted_iota(jnp.int32, (tm, LANES), 1)
    first_half = (lane % DIFF_HEAD_DIM) < (DIFF_HEAD_DIM // 2)

    def rope(tt, scale):
        outs = []
        for j in range(attn_width // LANES):
            tj = tt[:, j * LANES:(j + 1) * LANES]
            partner = jnp.where(first_half,
                                pltpu.roll(tj, LANES - DIFF_HEAD_DIM // 2, axis=1),
                                pltpu.roll(tj, DIFF_HEAD_DIM // 2, axis=1))
            outs.append((tj * cos + partner * sin) * scale)
        return jnp.concatenate(outs, axis=1)

    q0 = pool_width
    k0 = pool_width + attn_width
    v0 = pool_width + 2 * attn_width
    q = _dot(h, w_ref[:, q0:k0])
    q_ref[...] = rope(q, DIFF_HEAD_DIM ** -0.5 * LOG2_E).astype(q_ref.dtype)
    k = _dot(h, w_ref[:, k0:v0])
    k_ref[...] = rope(k, 1.0).astype(k_ref.dtype)
    v_ref[0] = _dot(h, w_ref[:, v0:v0 + attn_width]).T.astype(v_ref.dtype)


def _inproj(x2, pos_b, invf, g, w, pool_w, pool_scale, *, seq, tm):
    T, d_model = x2.shape
    pool_width = pool_scale.shape[-1]
    attn_width = (w.shape[1] - pool_width) // 3
    n_groups = pool_w.shape[0]
    kern = functools.partial(_inproj_kernel, tm=tm, tiles_per_seq=seq // tm,
                             pool_width=pool_width, attn_width=attn_width)
    row = lambda i: (i, 0)
    const2 = lambda i: (0, 0)
    out_sds = lambda width: jax.ShapeDtypeStruct((T, width), BF16)
    return pl.pallas_call(
        kern,
        grid=(T // tm,),
        in_specs=[
            pl.BlockSpec((tm, d_model), row),
            pl.BlockSpec((tm, LANES), row),
            pl.BlockSpec((1, LANES), const2),
            pl.BlockSpec((1, d_model), const2),
            pl.BlockSpec(w.shape, const2),
            pl.BlockSpec(pool_w.shape, lambda i: (0, 0, 0)),
            pl.BlockSpec((1, pool_width), const2),
        ],
        out_specs=[
            pl.BlockSpec((tm, pool_width), row),
            pl.BlockSpec((tm, attn_width), row),
            pl.BlockSpec((tm, attn_width), row),
            pl.BlockSpec((1, attn_width, tm), lambda i: (i, 0, 0)),
        ],
        out_shape=[out_sds(pool_width), out_sds(attn_width), out_sds(attn_width),
                   jax.ShapeDtypeStruct((T // tm, attn_width, tm), BF16)],
        scratch_shapes=[pltpu.VMEM((POOL_HALO, n_groups * POOL_GROUP_DIM), F32)],
        compiler_params=pltpu.CompilerParams(
            dimension_semantics=("arbitrary",),
            vmem_limit_bytes=48 * 1024 * 1024),
        name="inproj_pool_rope",
    )(x2, pos_b, invf, g, w, pool_w, pool_scale)


def _attn_kernel(lam_ref, g_ref, q_ref, k_ref, vt_ref, o_ref, m_ref, acc_ref, sa_ref, sb_ref,
                 *, tq, tk, heads, lam_init):
    i = pl.program_id(2)
    cols = 2 * tq
    groups = tk // SUBLANES

    sub = lax.broadcasted_iota(jnp.int32, (V_HEAD_DIM, tq), 0)
    qqts = []
    for hh in range(heads):
        qt = q_ref[:, hh * V_HEAD_DIM:(hh + 1) * V_HEAD_DIM].astype(F32).T
        zero = jnp.zeros_like(qt)
        qqts.append(jnp.concatenate([jnp.where(sub < DIFF_HEAD_DIM, qt, zero),
                                     jnp.where(sub >= DIFF_HEAD_DIM, qt, zero)],
                                    axis=1).astype(BF16))

    m_ref[...] = jnp.full(m_ref.shape, -jnp.inf, F32)
    acc_ref[...] = jnp.zeros(acc_ref.shape, F32)

    def scores(hh, j):
        ch = slice(hh * V_HEAD_DIM, (hh + 1) * V_HEAD_DIM)
        kt = k_ref[pl.ds(pl.multiple_of(j * tk, tk), tk), ch]
        return _dot(kt, qqts[hh])

    def softmax_pv(hh, j, s, masked):
        ch = slice(hh * V_HEAD_DIM, (hh + 1) * V_HEAD_DIM)
        if masked:
            c = lax.broadcasted_iota(jnp.int32, (SUBLANES, cols), 1)
            q_chunk = (i * tq - j * tk + jnp.where(c >= tq, c - tq, c)) >> CHUNK_LOG2
            blocks = []
            for kc in range(tk // CHUNK):
                blk = s[kc * CHUNK:(kc + 1) * CHUNK].reshape(CHUNK // SUBLANES, SUBLANES, cols)
                blocks.append(jnp.where((q_chunk >= kc)[None], blk, -jnp.inf).reshape(CHUNK, cols))
            s = jnp.concatenate(blocks, axis=0)
        s4 = s.reshape(MAX_WAYS, groups // MAX_WAYS, SUBLANES, cols)
        m_tile = jnp.max(jnp.max(jnp.max(s4, axis=1), axis=0), axis=0, keepdims=True)
        m_prev = m_ref[hh]
        m_new = jnp.maximum(m_prev, m_tile)
        alpha = jnp.exp2(m_prev - m_new)
        p = jnp.exp2(s.reshape(groups, SUBLANES, cols) - m_new[None]).reshape(tk, cols).astype(BF16)
        lhs = jnp.concatenate([vt_ref[j, ch, :], jnp.ones((ONES_ROWS, tk), BF16)], axis=0)
        pv = _dot(lhs, p)
        acc3 = acc_ref[hh].reshape(ACC_ROWS // SUBLANES, SUBLANES, cols)
        acc_ref[hh] = (alpha[None] * acc3).reshape(ACC_ROWS, cols) + pv
        m_ref[hh] = m_new

    def store_scores(j, buf):
        for hh in range(heads):
            buf[hh] = scores(hh, j)

    def full_step(j, cur, nxt):
        store_scores(j + 1, nxt)
        for hh in range(heads):
            softmax_pv(hh, j, cur[hh], masked=False)

    def last_step(j, cur):
        for hh in range(heads):
            softmax_pv(hh, j, cur[hh], masked=True)

    n_full = (i * tq) // tk
    store_scores(0, sa_ref)

    def pair(t, carry):
        full_step(2 * t, sa_ref, sb_ref)
        full_step(2 * t + 1, sb_ref, sa_ref)
        return carry

    lax.fori_loop(0, n_full // 2, pair, 0)

    @pl.when(n_full % 2 == 1)
    def _():
        full_step(n_full - 1, sa_ref, sb_ref)
        last_step(n_full, sb_ref)

    @pl.when(n_full % 2 == 0)
    def _():
        last_step(n_full, sa_ref)

    lam_vec = lam_ref[...]
    lam = (jnp.exp(jnp.sum(lam_vec[0:1] * lam_vec[1:2], axis=1, keepdims=True))
           - jnp.exp(jnp.sum(lam_vec[2:3] * lam_vec[3:4], axis=1, keepdims=True))
           + lam_init)
    for hh in range(heads):
        acc = acc_ref[hh]
        ot = acc[:V_HEAD_DIM] / acc[V_HEAD_DIM:V_HEAD_DIM + 1]
        od = (ot[:, :tq] - lam * ot[:, tq:]).T
        y = od * lax.rsqrt(jnp.mean(od * od, axis=-1, keepdims=True) + EPS)
        o_ref[:, hh * V_HEAD_DIM:(hh + 1) * V_HEAD_DIM] = (
            (y * g_ref[hh]) * (1.0 - lam_init)).astype(o_ref.dtype)


def _diff_attention(q, k, vt, lam_vec, head_g, *, batch, seq, lam_init, tq, heads):
    T, attn_width = q.shape
    tk = vt.shape[2]
    n_heads = attn_width // V_HEAD_DIM
    nq = seq // tq
    nk = seq // tk
    width = heads * V_HEAD_DIM
    kern = functools.partial(_attn_kernel, tq=tq, tk=tk, heads=heads, lam_init=lam_init)
    q_spec = pl.BlockSpec((tq, width), lambda b, h, i: (b * nq + i, h))
    return pl.pallas_call(
        kern,
        grid=(batch, n_heads // heads, nq),
        in_specs=[
            pl.BlockSpec(lam_vec.shape, lambda b, h, i: (0, 0)),
            pl.BlockSpec((heads, 1, V_HEAD_DIM), lambda b, h, i: (h, 0, 0)),
            q_spec,
            pl.BlockSpec((seq, width), lambda b, h, i: (b, h)),
            pl.BlockSpec((nk, width, tk), lambda b, h, i: (b, h, 0)),
        ],
        out_specs=q_spec,
        out_shape=jax.ShapeDtypeStruct((T, attn_width), BF16),
        scratch_shapes=[
            pltpu.VMEM((heads, SUBLANES, 2 * tq), F32),
            pltpu.VMEM((heads, ACC_ROWS, 2 * tq), F32),
            pltpu.VMEM((heads, tk, 2 * tq), F32),
            pltpu.VMEM((heads, tk, 2 * tq), F32),
        ],
        compiler_params=pltpu.CompilerParams(
            dimension_semantics=("arbitrary", "arbitrary", "arbitrary"),
            vmem_limit_bytes=32 * 1024 * 1024),
        name="diff_attention",
    )(lam_vec, head_g, q, k, vt)


def _ffn_kernel(x_ref, pool_ref, attn_ref, wo_ref, gf_ref, wup_ref, cw_ref, cb_ref, wd_ref,
                gl_ref, o_ref, carry_ref, x1_ref, h_ref, acc_ref, u00_ref, u01_ref, u10_ref, u11_ref,
                *, tm, slab, chunk, tiles_per_seq, n_chunks, apply_final):
    u_refs = ((u00_ref, u01_ref), (u10_ref, u11_ref))
    seq_tile = pl.program_id(0) % tiles_per_seq

    @pl.when(seq_tile == 0)
    def _():
        carry_ref[...] = jnp.zeros_like(carry_ref)

    mixed = jnp.concatenate([pool_ref[...], attn_ref[...]], axis=1)
    x1 = x_ref[...] + _dot(mixed, wo_ref[...])
    x1_ref[...] = x1
    ms = jnp.mean(x1 * x1, axis=-1, keepdims=True)
    h_ref[...] = ((x1 * lax.rsqrt(ms + EPS)) * gf_ref[...]).astype(BF16)
    acc_ref[...] = jnp.zeros_like(acc_ref)

    n_slabs = tm // slab
    cols = lambda idx: slice(idx * chunk, (idx + 1) * chunk)

    def up_matmul(c, r):
        h = h_ref[r * slab:(r + 1) * slab, :]
        return [_dot(h, wup_ref[:, cols(idx)]) for idx in (c, n_chunks + c)]

    def up_store(c, slot, r, us):
        for br, idx in enumerate((c, n_chunks + c)):
            u_ref = u_refs[slot][br]
            if r == 0:
                u_ref[0:CONV_HALO, :] = carry_ref[:, cols(idx)]
            u_ref[CONV_HALO + r * slab:CONV_HALO + (r + 1) * slab, :] = us[br]
            if r == n_slabs - 1:
                carry_ref[:, cols(idx)] = u_ref[tm:, :]

    def conv_branch(slot, br, idx, r):
        r0 = CONV_HALO + r * slab
        u_ref = u_refs[slot][br]
        u = u_ref[r0:r0 + slab, :]
        u1 = u_ref[r0 - 1:r0 - 1 + slab, :]
        u2 = u_ref[r0 - 2:r0 - 2 + slab, :]
        cw = cw_ref[:, cols(idx)]
        return ((cb_ref[:, cols(idx)] + cw[0:1] * u2) + cw[1:2] * u1) + cw[2:3] * u

    for r in range(n_slabs):
        up_store(0, 0, r, up_matmul(0, r))
    for c in range(n_chunks):
        slot = c % 2
        for r in range(n_slabs):
            nxt = up_matmul(c + 1, r) if c + 1 < n_chunks else None
            gate = conv_branch(slot, 0, c, r)
            val = conv_branch(slot, 1, n_chunks + c, r)
            act = (gate * (1.0 / (1.0 + jnp.exp(-gate)))) * val
            acc_ref[r * slab:(r + 1) * slab, :] += _dot(act.astype(BF16), wd_ref[cols(c), :])
            if nxt is not None:
                up_store(c + 1, 1 - slot, r, nxt)

    x2 = x1_ref[...] + acc_ref[...]
    if apply_final:
        ms2 = jnp.mean(x2 * x2, axis=-1, keepdims=True)
        x2 = (x2 * lax.rsqrt(ms2 + EPS)) * gl_ref[...]
    o_ref[...] = x2


def _ffn(x2d, pool_out, attn_out, wo, g_ffn, wup, cw, cb, wd, g_final,
         *, seq, tm, chunk, apply_final):
    T, d_model = x2d.shape
    n_chunks = wd.shape[0] // chunk
    kern = functools.partial(_ffn_kernel, tm=tm, tiles_per_seq=seq // tm,
                             slab=MXU_DIM, chunk=chunk, n_chunks=n_chunks, apply_final=apply_final)
    row = lambda i: (i, 0)
    const2 = lambda i: (0, 0)
    resident = dict(pipeline_mode=pl.Buffered(1))
    return pl.pallas_call(
        kern,
        grid=(T // tm,),
        in_specs=[
            pl.BlockSpec((tm, d_model), row),
            pl.BlockSpec((tm, pool_out.shape[1]), row),
            pl.BlockSpec((tm, attn_out.shape[1]), row),
            pl.BlockSpec(wo.shape, const2, **resident),
            pl.BlockSpec((1, d_model), const2),
            pl.BlockSpec(wup.shape, const2, **resident),
            pl.BlockSpec(cw.shape, const2),
            pl.BlockSpec(cb.shape, const2),
            pl.BlockSpec(wd.shape, const2, **resident),
            pl.BlockSpec((1, d_model), const2),
        ],
        out_specs=pl.BlockSpec((tm, d_model), row),
        out_shape=jax.ShapeDtypeStruct((T, d_model), F32),
        scratch_shapes=[
            pltpu.VMEM((CONV_HALO, 2 * n_chunks * chunk), F32),
            pltpu.VMEM((tm, d_model), F32),
            pltpu.VMEM((tm, d_model), BF16),
            pltpu.VMEM((tm, d_model), F32),
        ] + [pltpu.VMEM((CONV_HALO + tm, chunk), F32)] * 4,
        compiler_params=pltpu.CompilerParams(
            dimension_semantics=("arbitrary",),
            vmem_limit_bytes=56 * 1024 * 1024),
        name="outproj_convffn",
    )(x2d, pool_out, attn_out, wo, g_ffn, wup, cw, cb, wd, g_final)


def kernel(x, positions, norm_mix_g, w_in, pool_w, pool_scale, lambda_q1, lambda_k1, lambda_q2,
           lambda_k2, attn_norm_g, w_o, norm_ffn_g, w_up, conv_w, conv_b, w_down, norm_final_g):
    batch, seq, d_model = x.shape
    depth = w_in.shape[0]
    T = batch * seq
    d_ff = w_down.shape[1]
    n_heads = attn_norm_g.shape[1]
    chunk = MXU_DIM
    assert d_ff % chunk == 0 and seq % 512 == 0

    x2d = x.reshape(T, d_model)
    pos_b = jnp.broadcast_to(positions.astype(F32).reshape(T, 1), (T, LANES))
    inv_freq = ROPE_THETA ** (-jnp.arange(0, DIFF_HEAD_DIM, 2, dtype=F32) / DIFF_HEAD_DIM)
    invf = jnp.concatenate([-inv_freq, inv_freq, -inv_freq, inv_freq]).reshape(1, LANES)

    for l in range(depth):
        lam_init = 0.8 - 0.6 * float(np.exp(-0.3 * l))
        pool_out, q, k, vt = _inproj(
            x2d, pos_b, invf, norm_mix_g[l].reshape(1, d_model), w_in[l].astype(BF16),
            pool_w[l].astype(BF16), pool_scale[l].reshape(1, -1), seq=seq, tm=512)
        lam_vec = jnp.stack([lambda_q1[l], lambda_k1[l], lambda_q2[l], lambda_k2[l]]).astype(F32)
        attn_out = _diff_attention(
            q, k, vt, lam_vec, attn_norm_g[l].reshape(n_heads, 1, V_HEAD_DIM),
            batch=batch, seq=seq, lam_init=lam_init, tq=256, heads=2)
        x2d = _ffn(
            x2d, pool_out, attn_out, w_o[l].astype(BF16), norm_ffn_g[l].reshape(1, d_model),
            w_up[l].astype(BF16), conv_w[l], conv_b[l].reshape(1, -1), w_down[l].astype(BF16),
            norm_final_g.reshape(1, d_model),
            seq=seq, tm=512, chunk=chunk, apply_final=(l == depth - 1))
    return x2d.reshape(batch, seq, d_model)
```

```python
import functools

import numpy as np
import jax
import jax.numpy as jnp
from jax import lax
from jax.experimental import pallas as pl
from jax.experimental.pallas import tpu as pltpu

CHUNK = 64
CHUNK_LOG2 = CHUNK.bit_length() - 1
POOL_WINDOWS = (2, 4, 8, 16)
POOL_GROUP_DIM = 128
DIFF_HEAD_DIM = 64
V_HEAD_DIM = 2 * DIFF_HEAD_DIM
CONV_WIDTH = 3
ROPE_THETA = 10000.0
EPS = 1e-6
LOG2_E = 1.4426950408889634

LANES = 128
SUBLANES = 8
MXU_DIM = 256

POOL_HALO = 16
CONV_HALO = SUBLANES
ONES_ROWS = 16
ACC_ROWS = V_HEAD_DIM + ONES_ROWS
MAX_WAYS = 8

BF16 = jnp.bfloat16
F32 = jnp.float32


def _dot(a, b):
    return jnp.dot(a, b, preferred_element_type=F32)


def _shift_rows(a, k):
    return pltpu.roll(a, k, axis=0)


def _inproj_kernel(x_ref, pos_ref, invf_ref, g_ref, w_ref, pw_ref, ps_ref,
                   pool_ref, q_ref, k_ref, v_ref, carry_ref, *, tm, tiles_per_seq,
                   pool_width, attn_width):
    seq_tile = pl.program_id(0) % tiles_per_seq

    @pl.when(seq_tile == 0)
    def _():
        carry_ref[...] = jnp.zeros_like(carry_ref)

    x = x_ref[...]
    ms = jnp.mean(x * x, axis=-1, keepdims=True)
    h = ((x * lax.rsqrt(ms + EPS)) * g_ref[...]).astype(BF16)

    p = _dot(h, w_ref[:, 0:pool_width])
    t = seq_tile * tm + lax.broadcasted_iota(jnp.int32, (tm, POOL_GROUP_DIM), 0)
    pooled_out = []
    for g, w in enumerate(POOL_WINDOWS):
        cols = slice(g * POOL_GROUP_DIM, (g + 1) * POOL_GROUP_DIM)
        pg = p[:, cols]
        s = jnp.concatenate([carry_ref[:, cols], pg], axis=0)
        span = 1
        while span < w:
            s = s + _shift_rows(s, span)
            span *= 2
        count = jnp.minimum(t + 1, w).astype(F32)
        d = (s[POOL_HALO:] / count - pg).astype(BF16)
        pooled_out.append(_dot(d, pw_ref[g]))
    carry_ref[...] = p[tm - POOL_HALO:, :]
    pool_ref[...] = (jnp.concatenate(pooled_out, axis=1) * ps_ref[...]).astype(pool_ref.dtype)

    ang = pos_ref[...] * invf_ref[...]
    cos = jnp.cos(ang)
    sin = jnp.sin(ang)
    lane = lax.broadcasted_iota(jnp.int32, (tm, LANES), 1)
    first_half = (lane % DIFF_HEAD_DIM) < (DIFF_HEAD_DIM // 2)

    def rope(tt, scale):
        outs = []
        for j in range(attn_width // LANES):
            tj = tt[:, j * LANES:(j + 1) * LANES]
            partner = jnp.where(first_half,
                                pltpu.roll(tj, LANES - DIFF_HEAD_DIM // 2, axis=1),
                                pltpu.roll(tj, DIFF_HEAD_DIM // 2, axis=1))
            outs.append((tj * cos + partner * sin) * scale)
        return jnp.concatenate(outs, axis=1)

    q0 = pool_width
    k0 = pool_width + attn_width
    v0 = pool_width + 2 * attn_width
    q = _dot(h, w_ref[:, q0:k0])
    q_ref[...] = rope(q, DIFF_HEAD_DIM ** -0.5 * LOG2_E).astype(q_ref.dtype)
    k = _dot(h, w_ref[:, k0:v0])
    k_ref[...] = rope(k, 1.0).astype(k_ref.dtype)
    v_ref[0] = _dot(h, w_ref[:, v0:v0 + attn_width]).T.astype(v_ref.dtype)


def _inproj(x2, pos_b, invf, g, w, pool_w, pool_scale, *, seq, tm):
    T, d_model = x2.shape
    pool_width = pool_scale.shape[-1]
    attn_width = (w.shape[1] - pool_width) // 3
    n_groups = pool_w.shape[0]
    kern = functools.partial(_inproj_kernel, tm=tm, tiles_per_seq=seq // tm,
                             pool_width=pool_width, attn_width=attn_width)
    row = lambda i: (i, 0)
    const2 = lambda i: (0, 0)
    out_sds = lambda width: jax.ShapeDtypeStruct((T, width), BF16)
    return pl.pallas_call(
        kern,
        grid=(T // tm,),
        in_specs=[
            pl.BlockSpec((tm, d_model), row),
            pl.BlockSpec((tm, LANES), row),
            pl.BlockSpec((1, LANES), const2),
            pl.BlockSpec((1, d_model), const2),
            pl.BlockSpec(w.shape, const2),
            pl.BlockSpec(pool_w.shape, lambda i: (0, 0, 0)),
            pl.BlockSpec((1, pool_width), const2),
        ],
        out_specs=[
            pl.BlockSpec((tm, pool_width), row),
            pl.BlockSpec((tm, attn_width), row),
            pl.BlockSpec((tm, attn_width), row),
            pl.BlockSpec((1, attn_width, tm), lambda i: (i, 0, 0)),
        ],
        out_shape=[out_sds(pool_width), out_sds(attn_width), out_sds(attn_width),
                   jax.ShapeDtypeStruct((T // tm, attn_width, tm), BF16)],
        scratch_shapes=[pltpu.VMEM((POOL_HALO, n_groups * POOL_GROUP_DIM), F32)],
        compiler_params=pltpu.CompilerParams(
            dimension_semantics=("arbitrary",),
            vmem_limit_bytes=48 * 1024 * 1024),
        name="inproj_pool_rope",
    )(x2, pos_b, invf, g, w, pool_w, pool_scale)


def _attn_kernel(lam_ref, g_ref, q_ref, k_ref, vt_ref, o_ref, m_ref, acc_ref, sa_ref, sb_ref,
                 *, tq, tk, heads, lam_init):
    i = pl.program_id(2)
    cols = 2 * tq
    groups = tk // SUBLANES

    sub = lax.broadcasted_iota(jnp.int32, (V_HEAD_DIM, tq), 0)
    qqts = []
    for hh in range(heads):
        qt = q_ref[:, hh * V_HEAD_DIM:(hh + 1) * V_HEAD_DIM].astype(F32).T
        zero = jnp.zeros_like(qt)
        qqts.append(jnp.concatenate([jnp.where(sub < DIFF_HEAD_DIM, qt, zero),
                                     jnp.where(sub >= DIFF_HEAD_DIM, qt, zero)],
                                    axis=1).astype(BF16))

    m_ref[...] = jnp.full(m_ref.shape, -jnp.inf, F32)
    acc_ref[...] = jnp.zeros(acc_ref.shape, F32)

    def scores(hh, j):
        ch = slice(hh * V_HEAD_DIM, (hh + 1) * V_HEAD_DIM)
        kt = k_ref[pl.ds(pl.multiple_of(j * tk, tk), tk), ch]
        return _dot(kt, qqts[hh])

    def softmax_pv(hh, j, s, masked):
        ch = slice(hh * V_HEAD_DIM, (hh + 1) * V_HEAD_DIM)
        if masked:
            c = lax.broadcasted_iota(jnp.int32, (SUBLANES, cols), 1)
            q_chunk = (i * tq - j * tk + jnp.where(c >= tq, c - tq, c)) >> CHUNK_LOG2
            blocks = []
            for kc in range(tk // CHUNK):
                blk = s[kc * CHUNK:(kc + 1) * CHUNK].reshape(CHUNK // SUBLANES, SUBLANES, cols)
                blocks.append(jnp.where((q_chunk >= kc)[None], blk, -jnp.inf).reshape(CHUNK, cols))
            s = jnp.concatenate(blocks, axis=0)
        s4 = s.reshape(MAX_WAYS, groups // MAX_WAYS, SUBLANES, cols)
        m_tile = jnp.max(jnp.max(jnp.max(s4, axis=1), axis=0), axis=0, keepdims=True)
        m_prev = m_ref[hh]
        m_new = jnp.maximum(m_prev, m_tile)
        alpha = jnp.exp2(m_prev - m_new)
        p = jnp.exp2(s.reshape(groups, SUBLANES, cols) - m_new[None]).reshape(tk, cols).astype(BF16)
        lhs = jnp.concatenate([vt_ref[j, ch, :], jnp.ones((ONES_ROWS, tk), BF16)], axis=0)
        pv = _dot(lhs, p)
        acc3 = acc_ref[hh].reshape(ACC_ROWS // SUBLANES, SUBLANES, cols)
        acc_ref[hh] = (alpha[None] * acc3).reshape(ACC_ROWS, cols) + pv
        m_ref[hh] = m_new

    def store_scores(j, buf):
        for hh in range(heads):
            buf[hh] = scores(hh, j)

    def full_step(j, cur, nxt):
        store_scores(j + 1, nxt)
        for hh in range(heads):
            softmax_pv(hh, j, cur[hh], masked=False)

    def last_step(j, cur):
        for hh in range(heads):
            softmax_pv(hh, j, cur[hh], masked=True)

    n_full = (i * tq) // tk
    store_scores(0, sa_ref)

    def pair(t, carry):
        full_step(2 * t, sa_ref, sb_ref)
        full_step(2 * t + 1, sb_ref, sa_ref)
        return carry

    lax.fori_loop(0, n_full // 2, pair, 0)

    @pl.when(n_full % 2 == 1)
    def _():
        full_step(n_full - 1, sa_ref, sb_ref)
        last_step(n_full, sb_ref)

    @pl.when(n_full % 2 == 0)
    def _():
        last_step(n_full, sa_ref)

    lam_vec = lam_ref[...]
    lam = (jnp.exp(jnp.sum(lam_vec[0:1] * lam_vec[1:2], axis=1, keepdims=True))
           - jnp.exp(jnp.sum(lam_vec[2:3] * lam_vec[3:4], axis=1, keepdims=True))
           + lam_init)
    for hh in range(heads):
        acc = acc_ref[hh]
        ot = acc[:V_HEAD_DIM] / acc[V_HEAD_DIM:V_HEAD_DIM + 1]
        od = (ot[:, :tq] - lam * ot[:, tq:]).T
        y = od * lax.rsqrt(jnp.mean(od * od, axis=-1, keepdims=True) + EPS)
        o_ref[:, hh * V_HEAD_DIM:(hh + 1) * V_HEAD_DIM] = (
            (y * g_ref[hh]) * (1.0 - lam_init)).astype(o_ref.dtype)


def _diff_attention(q, k, vt, lam_vec, head_g, *, batch, seq, lam_init, tq, heads):
    T, attn_width = q.shape
    tk = vt.shape[2]
    n_heads = attn_width // V_HEAD_DIM
    nq = seq // tq
    nk = seq // tk
    width = heads * V_HEAD_DIM
    kern = functools.partial(_attn_kernel, tq=tq, tk=tk, heads=heads, lam_init=lam_init)
    q_spec = pl.BlockSpec((tq, width), lambda b, h, i: (b * nq + i, h))
    return pl.pallas_call(
        kern,
        grid=(batch, n_heads // heads, nq),
        in_specs=[
            pl.BlockSpec(lam_vec.shape, lambda b, h, i: (0, 0)),
            pl.BlockSpec((heads, 1, V_HEAD_DIM), lambda b, h, i: (h, 0, 0)),
            q_spec,
            pl.BlockSpec((seq, width), lambda b, h, i: (b, h)),
            pl.BlockSpec((nk, width, tk), lambda b, h, i: (b, h, 0)),
        ],
        out_specs=q_spec,
        out_shape=jax.ShapeDtypeStruct((T, attn_width), BF16),
        scratch_shapes=[
            pltpu.VMEM((heads, SUBLANES, 2 * tq), F32),
            pltpu.VMEM((heads, ACC_ROWS, 2 * tq), F32),
            pltpu.VMEM((heads, tk, 2 * tq), F32),
            pltpu.VMEM((heads, tk, 2 * tq), F32),
        ],
        compiler_params=pltpu.CompilerParams(
            dimension_semantics=("arbitrary", "arbitrary", "arbitrary"),
            vmem_limit_bytes=32 * 1024 * 1024),
        name="diff_attention",
    )(lam_vec, head_g, q, k, vt)


def _ffn_kernel(x_ref, pool_ref, attn_ref, wo_ref, gf_ref, wup_ref, cw_ref, cb_ref, wd_ref,
                gl_ref, o_ref, carry_ref, x1_ref, h_ref, a_ref, u00_ref, u01_ref, u10_ref, u11_ref,
                *, tm, slab, chunk, tiles_per_seq, n_chunks, apply_final):
    u_refs = ((u00_ref, u01_ref), (u10_ref, u11_ref))
    seq_tile = pl.program_id(0) % tiles_per_seq

    @pl.when(seq_tile == 0)
    def _():
        carry_ref[...] = jnp.zeros_like(carry_ref)

    mixed = jnp.concatenate([pool_ref[...], attn_ref[...]], axis=1)
    x1 = x_ref[...] + _dot(mixed, wo_ref[...])
    x1_ref[...] = x1
    ms = jnp.mean(x1 * x1, axis=-1, keepdims=True)
    h_ref[...] = ((x1 * lax.rsqrt(ms + EPS)) * gf_ref[...]).astype(BF16)

    n_slabs = tm // slab
    def cols(idx):
        if isinstance(idx, int):
            return slice(idx * chunk, (idx + 1) * chunk)
        return pl.ds(pl.multiple_of(idx * chunk, chunk), chunk)

    def up_matmul(c, r):
        h = h_ref[r * slab:(r + 1) * slab, :]
        return [_dot(h, wup_ref[:, cols(idx)]) for idx in (c, n_chunks + c)]

    def up_store(c, slot, r, us):
        for br, idx in enumerate((c, n_chunks + c)):
            u_ref = u_refs[slot][br]
            if r == 0:
                u_ref[0:CONV_HALO, :] = carry_ref[:, cols(idx)]
            u_ref[CONV_HALO + r * slab:CONV_HALO + (r + 1) * slab, :] = us[br]
            if r == n_slabs - 1:
                carry_ref[:, cols(idx)] = u_ref[tm:, :]

    def conv_branch(slot, br, idx, r):
        r0 = CONV_HALO + r * slab
        u_ref = u_refs[slot][br]
        u = u_ref[r0:r0 + slab, :]
        u1 = u_ref[r0 - 1:r0 - 1 + slab, :]
        u2 = u_ref[r0 - 2:r0 - 2 + slab, :]
        cw = cw_ref[:, cols(idx)]
        return ((cb_ref[:, cols(idx)] + cw[0:1] * u2) + cw[1:2] * u1) + cw[2:3] * u

    for r in range(n_slabs):
        up_store(0, 0, r, up_matmul(0, r))
    for c in range(n_chunks):
        slot = c % 2
        for r in range(n_slabs):
            nxt = up_matmul(c + 1, r) if c + 1 < n_chunks else None
            gate = conv_branch(slot, 0, c, r)
            val = conv_branch(slot, 1, n_chunks + c, r)
            act = (gate * jax.nn.sigmoid(gate)) * val
            a_ref[r * slab:(r + 1) * slab, cols(c)] = act.astype(BF16)
            if nxt is not None:
                up_store(c + 1, 1 - slot, r, nxt)
    ffn_out = _dot(a_ref[...], wd_ref[...])

    x2 = x1_ref[...] + ffn_out
    if apply_final:
        ms2 = jnp.mean(x2 * x2, axis=-1, keepdims=True)
        x2 = (x2 * lax.rsqrt(ms2 + EPS)) * gl_ref[...]
    o_ref[...] = x2


def _ffn(x2d, pool_out, attn_out, wo, g_ffn, wup, cw, cb, wd, g_final,
         *, seq, tm, chunk, apply_final):
    T, d_model = x2d.shape
    n_chunks = wd.shape[0] // chunk
    kern = functools.partial(_ffn_kernel, tm=tm, tiles_per_seq=seq // tm,
                             slab=MXU_DIM, chunk=chunk, n_chunks=n_chunks, apply_final=apply_final)
    row = lambda i: (i, 0)
    const2 = lambda i: (0, 0)
    resident = dict(pipeline_mode=pl.Buffered(1))
    return pl.pallas_call(
        kern,
        grid=(T // tm,),
        in_specs=[
            pl.BlockSpec((tm, d_model), row),
            pl.BlockSpec((tm, pool_out.shape[1]), row),
            pl.BlockSpec((tm, attn_out.shape[1]), row),
            pl.BlockSpec(wo.shape, const2, **resident),
            pl.BlockSpec((1, d_model), const2),
            pl.BlockSpec(wup.shape, const2, **resident),
            pl.BlockSpec(cw.shape, const2),
            pl.BlockSpec(cb.shape, const2),
            pl.BlockSpec(wd.shape, const2, **resident),
            pl.BlockSpec((1, d_model), const2),
        ],
        out_specs=pl.BlockSpec((tm, d_model), row),
        out_shape=jax.ShapeDtypeStruct((T, d_model), F32),
        scratch_shapes=[
            pltpu.VMEM((CONV_HALO, 2 * n_chunks * chunk), F32),
            pltpu.VMEM((tm, d_model), F32),
            pltpu.VMEM((tm, d_model), BF16),
            pltpu.VMEM((tm, n_chunks * chunk), BF16),
        ] + [pltpu.VMEM((CONV_HALO + tm, chunk), F32)] * 4,
        compiler_params=pltpu.CompilerParams(
            dimension_semantics=("arbitrary",),
            vmem_limit_bytes=56 * 1024 * 1024),
        name="outproj_convffn",
    )(x2d, pool_out, attn_out, wo, g_ffn, wup, cw, cb, wd, g_final)


def kernel(x, positions, norm_mix_g, w_in, pool_w, pool_scale, lambda_q1, lambda_k1, lambda_q2,
           lambda_k2, attn_norm_g, w_o, norm_ffn_g, w_up, conv_w, conv_b, w_down, norm_final_g):
    batch, seq, d_model = x.shape
    depth = w_in.shape[0]
    T = batch * seq
    d_ff = w_down.shape[1]
    n_heads = attn_norm_g.shape[1]
    chunk = MXU_DIM
    assert d_ff % chunk == 0 and seq % 512 == 0

    x2d = x.reshape(T, d_model)
    pos_b = jnp.broadcast_to(positions.astype(F32).reshape(T, 1), (T, LANES))
    inv_freq = ROPE_THETA ** (-jnp.arange(0, DIFF_HEAD_DIM, 2, dtype=F32) / DIFF_HEAD_DIM)
    invf = jnp.concatenate([-inv_freq, inv_freq, -inv_freq, inv_freq]).reshape(1, LANES)

    for l in range(depth):
        lam_init = 0.8 - 0.6 * float(np.exp(-0.3 * l))
        pool_out, q, k, vt = _inproj(
            x2d, pos_b, invf, norm_mix_g[l].reshape(1, d_model), w_in[l].astype(BF16),
            pool_w[l].astype(BF16), pool_scale[l].reshape(1, -1), seq=seq, tm=512)
        lam_vec = jnp.stack([lambda_q1[l], lambda_k1[l], lambda_q2[l], lambda_k2[l]]).astype(F32)
        attn_out = _diff_attention(
            q, k, vt, lam_vec, attn_norm_g[l].reshape(n_heads, 1, V_HEAD_DIM),
            batch=batch, seq=seq, lam_init=lam_init, tq=256, heads=2)
        x2d = _ffn(
            x2d, pool_out, attn_out, w_o[l].astype(BF16), norm_ffn_g[l].reshape(1, d_model),
            w_up[l].astype(BF16), conv_w[l], conv_b[l].reshape(1, -1), w_down[l].astype(BF16),
            norm_final_g.reshape(1, d_model),
            seq=seq, tm=512, chunk=chunk, apply_final=(l == depth - 1))
    return x2d.reshape(batch, seq, d_model)
```

```python
import functools

import numpy as np
import jax
import jax.numpy as jnp
from jax import lax
from jax.experimental import pallas as pl
from jax.experimental.pallas import tpu as pltpu

CHUNK = 64
CHUNK_LOG2 = CHUNK.bit_length() - 1
POOL_WINDOWS = (2, 4, 8, 16)
POOL_GROUP_DIM = 128
DIFF_HEAD_DIM = 64
V_HEAD_DIM = 2 * DIFF_HEAD_DIM
CONV_WIDTH = 3
ROPE_THETA = 10000.0
EPS = 1e-6
LOG2_E = 1.4426950408889634

LANES = 128
SUBLANES = 8
MXU_DIM = 256

POOL_HALO = 16
CONV_HALO = SUBLANES
ONES_ROWS = 16
ACC_ROWS = V_HEAD_DIM + ONES_ROWS
MAX_WAYS = 8

BF16 = jnp.bfloat16
F32 = jnp.float32


def _dot(a, b):
    return jnp.dot(a, b, preferred_element_type=F32)


def _shift_rows(a, k):
    return pltpu.roll(a, k, axis=0)


def _inproj_kernel(x_ref, pos_ref, invf_ref, g_ref, w_ref, pw_ref, ps_ref,
                   pool_ref, q_ref, k_ref, v_ref, carry_ref, *, tm, tiles_per_seq,
                   pool_width, attn_width):
    seq_tile = pl.program_id(0) % tiles_per_seq

    @pl.when(seq_tile == 0)
    def _():
        carry_ref[...] = jnp.zeros_like(carry_ref)

    x = x_ref[...]
    ms = jnp.mean(x * x, axis=-1, keepdims=True)
    h = ((x * lax.rsqrt(ms + EPS)) * g_ref[...]).astype(BF16)

    p = _dot(h, w_ref[:, 0:pool_width])
    t = seq_tile * tm + lax.broadcasted_iota(jnp.int32, (tm, POOL_GROUP_DIM), 0)
    pooled_out = []
    for g, w in enumerate(POOL_WINDOWS):
        cols = slice(g * POOL_GROUP_DIM, (g + 1) * POOL_GROUP_DIM)
        pg = p[:, cols]
        s = jnp.concatenate([carry_ref[:, cols], pg], axis=0)
        span = 1
        while span < w:
            s = s + _shift_rows(s, span)
            span *= 2
        count = jnp.minimum(t + 1, w).astype(F32)
        d = (s[POOL_HALO:] / count - pg).astype(BF16)
        pooled_out.append(_dot(d, pw_ref[g]))
    carry_ref[...] = p[tm - POOL_HALO:, :]
    pool_ref[...] = (jnp.concatenate(pooled_out, axis=1) * ps_ref[...]).astype(pool_ref.dtype)

    half = DIFF_HEAD_DIM // 2
    groups = LANES // half
    tq = tm // groups
    ang = pos_ref[...] * invf_ref[...]
    cos_c = jnp.cos(ang)
    sin_c = jnp.sin(ang)
    lane = lax.broadcasted_iota(jnp.int32, (tq, LANES), 1)
    lane_group = lane // half
    first_half = (lane % DIFF_HEAD_DIM) < half
    sign = jnp.where(first_half, -1.0, 1.0).astype(F32)

    def spread(x, a):
        y = None
        for g in range(groups):
            shift = ((g - a) * half) % LANES
            xr = x if shift == 0 else pltpu.roll(x, shift, axis=1)
            y = xr if y is None else jnp.where(lane_group == g, xr, y)
        return y

    tables = [(spread(cos_c, a), spread(sin_c, a) * sign) for a in range(groups)]

    def rope(tt, scale):
        blocks = []
        for a, (cos, sin) in enumerate(tables):
            outs = []
            for j in range(attn_width // LANES):
                tj = tt[a * tq:(a + 1) * tq, j * LANES:(j + 1) * LANES]
                partner = jnp.where(first_half,
                                    pltpu.roll(tj, LANES - half, axis=1),
                                    pltpu.roll(tj, half, axis=1))
                outs.append((tj * cos + partner * sin) * scale)
            blocks.append(jnp.concatenate(outs, axis=1))
        return jnp.concatenate(blocks, axis=0)

    q0 = pool_width
    k0 = pool_width + attn_width
    v0 = pool_width + 2 * attn_width
    q = _dot(h, w_ref[:, q0:k0])
    q_ref[...] = rope(q, DIFF_HEAD_DIM ** -0.5 * LOG2_E).astype(q_ref.dtype)
    k = _dot(h, w_ref[:, k0:v0])
    k_ref[...] = rope(k, 1.0).astype(k_ref.dtype)
    v_ref[0] = _dot(h, w_ref[:, v0:v0 + attn_width]).T.astype(v_ref.dtype)


def _inproj(x2, pos_b, invf, g, w, pool_w, pool_scale, *, seq, tm):
    T, d_model = x2.shape
    pool_width = pool_scale.shape[-1]
    attn_width = (w.shape[1] - pool_width) // 3
    n_groups = pool_w.shape[0]
    kern = functools.partial(_inproj_kernel, tm=tm, tiles_per_seq=seq // tm,
                             pool_width=pool_width, attn_width=attn_width)
    row = lambda i: (i, 0)
    const2 = lambda i: (0, 0)
    out_sds = lambda width: jax.ShapeDtypeStruct((T, width), BF16)
    return pl.pallas_call(
        kern,
        grid=(T // tm,),
        in_specs=[
            pl.BlockSpec((tm, d_model), row),
            pl.BlockSpec((tm * (DIFF_HEAD_DIM // 2) // LANES, LANES), row),
            pl.BlockSpec((1, LANES), const2),
            pl.BlockSpec((1, d_model), const2),
            pl.BlockSpec(w.shape, const2),
            pl.BlockSpec(pool_w.shape, lambda i: (0, 0, 0)),
            pl.BlockSpec((1, pool_width), const2),
        ],
        out_specs=[
            pl.BlockSpec((tm, pool_width), row),
            pl.BlockSpec((tm, attn_width), row),
            pl.BlockSpec((tm, attn_width), row),
            pl.BlockSpec((1, attn_width, tm), lambda i: (i, 0, 0)),
        ],
        out_shape=[out_sds(pool_width), out_sds(attn_width), out_sds(attn_width),
                   jax.ShapeDtypeStruct((T // tm, attn_width, tm), BF16)],
        scratch_shapes=[pltpu.VMEM((POOL_HALO, n_groups * POOL_GROUP_DIM), F32)],
        compiler_params=pltpu.CompilerParams(
            dimension_semantics=("arbitrary",),
            vmem_limit_bytes=48 * 1024 * 1024),
        name="inproj_pool_rope",
    )(x2, pos_b, invf, g, w, pool_w, pool_scale)


def _attn_kernel(lam_ref, g_ref, q_ref, k_ref, vt_ref, o_ref, m_ref, acc_ref, sa_ref, sb_ref,
                 *, tq, tk, heads, lam_init):
    i = pl.program_id(2)
    cols = 2 * tq
    groups = tk // SUBLANES

    sub = lax.broadcasted_iota(jnp.int32, (V_HEAD_DIM, tq), 0)
    qqts = []
    for hh in range(heads):
        qt = q_ref[:, hh * V_HEAD_DIM:(hh + 1) * V_HEAD_DIM].astype(F32).T
        zero = jnp.zeros_like(qt)
        qqts.append(jnp.concatenate([jnp.where(sub < DIFF_HEAD_DIM, qt, zero),
                                     jnp.where(sub >= DIFF_HEAD_DIM, qt, zero)],
                                    axis=1).astype(BF16))

    m_ref[...] = jnp.full(m_ref.shape, -jnp.inf, F32)
    acc_ref[...] = jnp.zeros(acc_ref.shape, F32)

    def scores(hh, j):
        ch = slice(hh * V_HEAD_DIM, (hh + 1) * V_HEAD_DIM)
        kt = k_ref[pl.ds(pl.multiple_of(j * tk, tk), tk), ch]
        return _dot(kt, qqts[hh])

    def softmax_pv(hh, j, s, masked):
        ch = slice(hh * V_HEAD_DIM, (hh + 1) * V_HEAD_DIM)
        if masked:
            c = lax.broadcasted_iota(jnp.int32, (SUBLANES, cols), 1)
            q_chunk = (i * tq - j * tk + jnp.where(c >= tq, c - tq, c)) >> CHUNK_LOG2
            blocks = []
            for kc in range(tk // CHUNK):
                blk = s[kc * CHUNK:(kc + 1) * CHUNK].reshape(CHUNK // SUBLANES, SUBLANES, cols)
                blocks.append(jnp.where((q_chunk >= kc)[None], blk, -jnp.inf).reshape(CHUNK, cols))
            s = jnp.concatenate(blocks, axis=0)
        s4 = s.reshape(MAX_WAYS, groups // MAX_WAYS, SUBLANES, cols)
        m_tile = jnp.max(jnp.max(jnp.max(s4, axis=1), axis=0), axis=0, keepdims=True)
        m_prev = m_ref[hh]
        m_new = jnp.maximum(m_prev, m_tile)
        alpha = jnp.exp2(m_prev - m_new)
        p = jnp.exp2(s.reshape(groups, SUBLANES, cols) - m_new[None]).reshape(tk, cols).astype(BF16)
        lhs = jnp.concatenate([vt_ref[j, ch, :], jnp.ones((ONES_ROWS, tk), BF16)], axis=0)
        pv = _dot(lhs, p)
        acc3 = acc_ref[hh].reshape(ACC_ROWS // SUBLANES, SUBLANES, cols)
        acc_ref[hh] = (alpha[None] * acc3).reshape(ACC_ROWS, cols) + pv
        m_ref[hh] = m_new

    def store_scores(j, buf):
        for hh in range(heads):
            buf[hh] = scores(hh, j)

    def full_step(j, cur, nxt):
        store_scores(j + 1, nxt)
        for hh in range(heads):
            softmax_pv(hh, j, cur[hh], masked=False)

    def last_step(j, cur):
        for hh in range(heads):
            softmax_pv(hh, j, cur[hh], masked=True)

    n_full = (i * tq) // tk
    store_scores(0, sa_ref)

    def pair(t, carry):
        full_step(2 * t, sa_ref, sb_ref)
        full_step(2 * t + 1, sb_ref, sa_ref)
        return carry

    lax.fori_loop(0, n_full // 2, pair, 0)

    @pl.when(n_full % 2 == 1)
    def _():
        full_step(n_full - 1, sa_ref, sb_ref)
        last_step(n_full, sb_ref)

    @pl.when(n_full % 2 == 0)
    def _():
        last_step(n_full, sa_ref)

    lam_vec = lam_ref[...]
    lam = (jnp.exp(jnp.sum(lam_vec[0:1] * lam_vec[1:2], axis=1, keepdims=True))
           - jnp.exp(jnp.sum(lam_vec[2:3] * lam_vec[3:4], axis=1, keepdims=True))
           + lam_init)
    for hh in range(heads):
        acc = acc_ref[hh]
        ot = acc[:V_HEAD_DIM] / acc[V_HEAD_DIM:V_HEAD_DIM + 1]
        od = (ot[:, :tq] - lam * ot[:, tq:]).T
        y = od * lax.rsqrt(jnp.mean(od * od, axis=-1, keepdims=True) + EPS)
        o_ref[:, hh * V_HEAD_DIM:(hh + 1) * V_HEAD_DIM] = (
            (y * g_ref[hh]) * (1.0 - lam_init)).astype(o_ref.dtype)


def _diff_attention(q, k, vt, lam_vec, head_g, *, batch, seq, lam_init, tq, heads):
    T, attn_width = q.shape
    tk = vt.shape[2]
    n_heads = attn_width // V_HEAD_DIM
    nq = seq // tq
    nk = seq // tk
    width = heads * V_HEAD_DIM
    kern = functools.partial(_attn_kernel, tq=tq, tk=tk, heads=heads, lam_init=lam_init)
    q_spec = pl.BlockSpec((tq, width), lambda b, h, i: (b * nq + i, h))
    return pl.pallas_call(
        kern,
        grid=(batch, n_heads // heads, nq),
        in_specs=[
            pl.BlockSpec(lam_vec.shape, lambda b, h, i: (0, 0)),
            pl.BlockSpec((heads, 1, V_HEAD_DIM), lambda b, h, i: (h, 0, 0)),
            q_spec,
            pl.BlockSpec((seq, width), lambda b, h, i: (b, h)),
            pl.BlockSpec((nk, width, tk), lambda b, h, i: (b, h, 0)),
        ],
        out_specs=q_spec,
        out_shape=jax.ShapeDtypeStruct((T, attn_width), BF16),
        scratch_shapes=[
            pltpu.VMEM((heads, SUBLANES, 2 * tq), F32),
            pltpu.VMEM((heads, ACC_ROWS, 2 * tq), F32),
            pltpu.VMEM((heads, tk, 2 * tq), F32),
            pltpu.VMEM((heads, tk, 2 * tq), F32),
        ],
        compiler_params=pltpu.CompilerParams(
            dimension_semantics=("arbitrary", "arbitrary", "arbitrary"),
            vmem_limit_bytes=32 * 1024 * 1024),
        name="diff_attention",
    )(lam_vec, head_g, q, k, vt)


def _ffn_kernel(x_ref, pool_ref, attn_ref, wo_ref, gf_ref, wup_ref, cw_ref, cb_ref, wd_ref,
                gl_ref, o_ref, carry_ref, x1_ref, h_ref, a_ref, u00_ref, u01_ref, u10_ref, u11_ref,
                *, tm, slab, chunk, tiles_per_seq, n_chunks, apply_final):
    u_refs = ((u00_ref, u01_ref), (u10_ref, u11_ref))
    seq_tile = pl.program_id(0) % tiles_per_seq

    @pl.when(seq_tile == 0)
    def _():
        carry_ref[...] = jnp.zeros_like(carry_ref)

    mixed = jnp.concatenate([pool_ref[...], attn_ref[...]], axis=1)
    x1 = x_ref[...] + _dot(mixed, wo_ref[...])
    x1_ref[...] = x1
    ms = jnp.mean(x1 * x1, axis=-1, keepdims=True)
    h_ref[...] = ((x1 * lax.rsqrt(ms + EPS)) * gf_ref[...]).astype(BF16)

    n_slabs = tm // slab
    cols = lambda idx: slice(idx * chunk, (idx + 1) * chunk)

    def up_matmul(c, r):
        h = h_ref[r * slab:(r + 1) * slab, :]
        return [_dot(h, wup_ref[:, cols(idx)]) for idx in (c, n_chunks + c)]

    def up_store(c, slot, r, us):
        for br, idx in enumerate((c, n_chunks + c)):
            u_ref = u_refs[slot][br]
            if r == 0:
                u_ref[0:CONV_HALO, :] = carry_ref[:, cols(idx)]
            u_ref[CONV_HALO + r * slab:CONV_HALO + (r + 1) * slab, :] = us[br]
            if r == n_slabs - 1:
                carry_ref[:, cols(idx)] = u_ref[tm:, :]

    def conv_branch(slot, br, idx, r):
        r0 = CONV_HALO + r * slab
        u_ref = u_refs[slot][br]
        u = u_ref[r0:r0 + slab, :]
        u1 = u_ref[r0 - 1:r0 - 1 + slab, :]
        u2 = u_ref[r0 - 2:r0 - 2 + slab, :]
        cw = cw_ref[:, cols(idx)]
        return ((cb_ref[:, cols(idx)] + cw[0:1] * u2) + cw[1:2] * u1) + cw[2:3] * u

    for r in range(n_slabs):
        up_store(0, 0, r, up_matmul(0, r))
    for c in range(n_chunks):
        slot = c % 2
        for r in range(n_slabs):
            nxt = up_matmul(c + 1, r) if c + 1 < n_chunks else None
            gate = conv_branch(slot, 0, c, r)
            val = conv_branch(slot, 1, n_chunks + c, r)
            act = (gate * jax.nn.sigmoid(gate)) * val
            a_ref[r * slab:(r + 1) * slab, cols(c)] = act.astype(BF16)
            if nxt is not None:
                up_store(c + 1, 1 - slot, r, nxt)
    ffn_out = _dot(a_ref[...], wd_ref[...])

    x2 = x1_ref[...] + ffn_out
    if apply_final:
        ms2 = jnp.mean(x2 * x2, axis=-1, keepdims=True)
        x2 = (x2 * lax.rsqrt(ms2 + EPS)) * gl_ref[...]
    o_ref[...] = x2


def _ffn(x2d, pool_out, attn_out, wo, g_ffn, wup, cw, cb, wd, g_final,
         *, seq, tm, chunk, apply_final):
    T, d_model = x2d.shape
    n_chunks = wd.shape[0] // chunk
    kern = functools.partial(_ffn_kernel, tm=tm, tiles_per_seq=seq // tm,
                             slab=MXU_DIM, chunk=chunk, n_chunks=n_chunks, apply_final=apply_final)
    row = lambda i: (i, 0)
    const2 = lambda i: (0, 0)
    resident = dict(pipeline_mode=pl.Buffered(1))
    return pl.pallas_call(
        kern,
        grid=(T // tm,),
        in_specs=[
            pl.BlockSpec((tm, d_model), row),
            pl.BlockSpec((tm, pool_out.shape[1]), row),
            pl.BlockSpec((tm, attn_out.shape[1]), row),
            pl.BlockSpec(wo.shape, const2, **resident),
            pl.BlockSpec((1, d_model), const2),
            pl.BlockSpec(wup.shape, const2, **resident),
            pl.BlockSpec(cw.shape, const2),
            pl.BlockSpec(cb.shape, const2),
            pl.BlockSpec(wd.shape, const2, **resident),
            pl.BlockSpec((1, d_model), const2),
        ],
        out_specs=pl.BlockSpec((tm, d_model), row),
        out_shape=jax.ShapeDtypeStruct((T, d_model), F32),
        scratch_shapes=[
            pltpu.VMEM((CONV_HALO, 2 * n_chunks * chunk), F32),
            pltpu.VMEM((tm, d_model), F32),
            pltpu.VMEM((tm, d_model), BF16),
            pltpu.VMEM((tm, n_chunks * chunk), BF16),
        ] + [pltpu.VMEM((CONV_HALO + tm, chunk), F32)] * 4,
        compiler_params=pltpu.CompilerParams(
            dimension_semantics=("arbitrary",),
            vmem_limit_bytes=56 * 1024 * 1024),
        name="outproj_convffn",
    )(x2d, pool_out, attn_out, wo, g_ffn, wup, cw, cb, wd, g_final)


def kernel(x, positions, norm_mix_g, w_in, pool_w, pool_scale, lambda_q1, lambda_k1, lambda_q2,
           lambda_k2, attn_norm_g, w_o, norm_ffn_g, w_up, conv_w, conv_b, w_down, norm_final_g):
    batch, seq, d_model = x.shape
    depth = w_in.shape[0]
    T = batch * seq
    d_ff = w_down.shape[1]
    n_heads = attn_norm_g.shape[1]
    chunk = MXU_DIM
    assert d_ff % chunk == 0 and seq % 512 == 0

    x2d = x.reshape(T, d_model)
    tm_in = 512
    half = DIFF_HEAD_DIM // 2
    groups = LANES // half
    pos_c = positions.astype(F32).reshape(T // tm_in, groups, tm_in // groups).transpose(0, 2, 1)
    pos_c = jnp.broadcast_to(pos_c[..., None], pos_c.shape + (half,)).reshape(T // groups, LANES)
    inv_freq = ROPE_THETA ** (-jnp.arange(0, DIFF_HEAD_DIM, 2, dtype=F32) / DIFF_HEAD_DIM)
    invf = jnp.tile(inv_freq, groups).reshape(1, LANES)

    for l in range(depth):
        lam_init = 0.8 - 0.6 * float(np.exp(-0.3 * l))
        pool_out, q, k, vt = _inproj(
            x2d, pos_c, invf, norm_mix_g[l].reshape(1, d_model), w_in[l].astype(BF16),
            pool_w[l].astype(BF16), pool_scale[l].reshape(1, -1), seq=seq, tm=tm_in)
        lam_vec = jnp.stack([lambda_q1[l], lambda_k1[l], lambda_q2[l], lambda_k2[l]]).astype(F32)
        attn_out = _diff_attention(
            q, k, vt, lam_vec, attn_norm_g[l].reshape(n_heads, 1, V_HEAD_DIM),
            batch=batch, seq=seq, lam_init=lam_init, tq=256, heads=2)
        x2d = _ffn(
            x2d, pool_out, attn_out, w_o[l].astype(BF16), norm_ffn_g[l].reshape(1, d_model),
            w_up[l].astype(BF16), conv_w[l], conv_b[l].reshape(1, -1), w_down[l].astype(BF16),
            norm_final_g.reshape(1, d_model),
            seq=seq, tm=512, chunk=chunk, apply_final=(l == depth - 1))
    return x2d.reshape(batch, seq, d_model)
```

```python
import functools

import numpy as np
import jax
import jax.numpy as jnp
from jax import lax
from jax.experimental import pallas as pl
from jax.experimental.pallas import tpu as pltpu

CHUNK = 64
CHUNK_LOG2 = CHUNK.bit_length() - 1
POOL_WINDOWS = (2, 4, 8, 16)
POOL_GROUP_DIM = 128
DIFF_HEAD_DIM = 64
V_HEAD_DIM = 2 * DIFF_HEAD_DIM
CONV_WIDTH = 3
ROPE_THETA = 10000.0
EPS = 1e-6
LOG2_E = 1.4426950408889634

LANES = 128
SUBLANES = 8
MXU_DIM = 256

POOL_HALO = 16
CONV_HALO = SUBLANES
ONES_ROWS = 16
ACC_ROWS = V_HEAD_DIM + ONES_ROWS
MAX_WAYS = 8

BF16 = jnp.bfloat16
F32 = jnp.float32


def _dot(a, b):
    return jnp.dot(a, b, preferred_element_type=F32)


def _shift_rows(a, k):
    return pltpu.roll(a, k, axis=0)


def _inproj_kernel(x_ref, pos_ref, invf_ref, g_ref, w_ref, pw_ref, ps_ref,
                   pool_ref, q_ref, k_ref, v_ref, carry_ref, *, tm, tiles_per_seq,
                   pool_width, attn_width):
    seq_tile = pl.program_id(0) % tiles_per_seq

    @pl.when(seq_tile == 0)
    def _():
        carry_ref[...] = jnp.zeros_like(carry_ref)

    x = x_ref[...]
    ms = jnp.mean(x * x, axis=-1, keepdims=True)
    h = ((x * lax.rsqrt(ms + EPS)) * g_ref[...]).astype(BF16)

    p = _dot(h, w_ref[:, 0:pool_width])
    t = seq_tile * tm + lax.broadcasted_iota(jnp.int32, (tm, POOL_GROUP_DIM), 0)
    pooled_out = []
    for g, w in enumerate(POOL_WINDOWS):
        cols = slice(g * POOL_GROUP_DIM, (g + 1) * POOL_GROUP_DIM)
        pg = p[:, cols]
        s = jnp.concatenate([carry_ref[:, cols], pg], axis=0)
        span = 1
        while span < w:
            s = s + _shift_rows(s, span)
            span *= 2
        count = jnp.minimum(t + 1, w).astype(F32)
        d = (s[POOL_HALO:] / count - pg).astype(BF16)
        pooled_out.append(_dot(d, pw_ref[g]))
    carry_ref[...] = p[tm - POOL_HALO:, :]
    pool_ref[...] = (jnp.concatenate(pooled_out, axis=1) * ps_ref[...]).astype(pool_ref.dtype)

    half = DIFF_HEAD_DIM // 2
    groups = LANES // half
    tq = tm // groups
    ang = pos_ref[...] * invf_ref[...]
    cos_c = jnp.cos(ang)
    sin_c = jnp.sin(ang)
    lane = lax.broadcasted_iota(jnp.int32, (tq, LANES), 1)
    lane_group = lane // half
    first_half = (lane % DIFF_HEAD_DIM) < half
    sign = jnp.where(first_half, -1.0, 1.0).astype(F32)

    def spread(x, a):
        y = None
        for g in range(groups):
            shift = ((g - a) * half) % LANES
            xr = x if shift == 0 else pltpu.roll(x, shift, axis=1)
            y = xr if y is None else jnp.where(lane_group == g, xr, y)
        return y

    tables = [(spread(cos_c, a), spread(sin_c, a) * sign) for a in range(groups)]

    def rope(tt, scale):
        blocks = []
        for a, (cos, sin) in enumerate(tables):
            outs = []
            for j in range(attn_width // LANES):
                tj = tt[a * tq:(a + 1) * tq, j * LANES:(j + 1) * LANES]
                partner = jnp.where(first_half,
                                    pltpu.roll(tj, LANES - half, axis=1),
                                    pltpu.roll(tj, half, axis=1))
                outs.append((tj * cos + partner * sin) * scale)
            blocks.append(jnp.concatenate(outs, axis=1))
        return jnp.concatenate(blocks, axis=0)

    q0 = pool_width
    k0 = pool_width + attn_width
    v0 = pool_width + 2 * attn_width
    q = _dot(h, w_ref[:, q0:k0])
    q_ref[...] = rope(q, DIFF_HEAD_DIM ** -0.5 * LOG2_E).astype(q_ref.dtype)
    k = _dot(h, w_ref[:, k0:v0])
    k_ref[...] = rope(k, 1.0).astype(k_ref.dtype)
    v_ref[0] = _dot(h, w_ref[:, v0:v0 + attn_width]).T.astype(v_ref.dtype)


def _inproj(x2, pos_b, invf, g, w, pool_w, pool_scale, *, seq, tm):
    T, d_model = x2.shape
    pool_width = pool_scale.shape[-1]
    attn_width = (w.shape[1] - pool_width) // 3
    n_groups = pool_w.shape[0]
    kern = functools.partial(_inproj_kernel, tm=tm, tiles_per_seq=seq // tm,
                             pool_width=pool_width, attn_width=attn_width)
    row = lambda i: (i, 0)
    const2 = lambda i: (0, 0)
    out_sds = lambda width: jax.ShapeDtypeStruct((T, width), BF16)
    return pl.pallas_call(
        kern,
        grid=(T // tm,),
        in_specs=[
            pl.BlockSpec((tm, d_model), row),
            pl.BlockSpec((tm * (DIFF_HEAD_DIM // 2) // LANES, LANES), row),
            pl.BlockSpec((1, LANES), const2),
            pl.BlockSpec((1, d_model), const2),
            pl.BlockSpec(w.shape, const2),
            pl.BlockSpec(pool_w.shape, lambda i: (0, 0, 0)),
            pl.BlockSpec((1, pool_width), const2),
        ],
        out_specs=[
            pl.BlockSpec((tm, pool_width), row),
            pl.BlockSpec((tm, attn_width), row),
            pl.BlockSpec((tm, attn_width), row),
            pl.BlockSpec((1, attn_width, tm), lambda i: (i, 0, 0)),
        ],
        out_shape=[out_sds(pool_width), out_sds(attn_width), out_sds(attn_width),
                   jax.ShapeDtypeStruct((T // tm, attn_width, tm), BF16)],
        scratch_shapes=[pltpu.VMEM((POOL_HALO, n_groups * POOL_GROUP_DIM), F32)],
        compiler_params=pltpu.CompilerParams(
            dimension_semantics=("arbitrary",),
            vmem_limit_bytes=48 * 1024 * 1024),
        name="inproj_pool_rope",
    )(x2, pos_b, invf, g, w, pool_w, pool_scale)


def _attn_kernel(lam_ref, g_ref, q_ref, k_ref, vt_ref, o_ref, m_ref, acc_ref, sa_ref, sb_ref,
                 *, tq, tk, heads, lam_init):
    i = pl.program_id(2)
    cols = 2 * tq
    groups = tk // SUBLANES

    sub = lax.broadcasted_iota(jnp.int32, (V_HEAD_DIM, tq), 0)
    qqts = []
    for hh in range(heads):
        qt = q_ref[:, hh * V_HEAD_DIM:(hh + 1) * V_HEAD_DIM].astype(F32).T
        zero = jnp.zeros_like(qt)
        qqts.append(jnp.concatenate([jnp.where(sub < DIFF_HEAD_DIM, qt, zero),
                                     jnp.where(sub >= DIFF_HEAD_DIM, qt, zero)],
                                    axis=1).astype(BF16))

    m_ref[...] = jnp.full(m_ref.shape, -jnp.inf, F32)
    acc_ref[...] = jnp.zeros(acc_ref.shape, F32)

    def scores(hh, j):
        ch = slice(hh * V_HEAD_DIM, (hh + 1) * V_HEAD_DIM)
        kt = k_ref[pl.ds(pl.multiple_of(j * tk, tk), tk), ch]
        return _dot(kt, qqts[hh])

    def softmax_pv(hh, j, s, masked):
        ch = slice(hh * V_HEAD_DIM, (hh + 1) * V_HEAD_DIM)
        if masked:
            c = lax.broadcasted_iota(jnp.int32, (SUBLANES, cols), 1)
            q_chunk = (i * tq - j * tk + jnp.where(c >= tq, c - tq, c)) >> CHUNK_LOG2
            blocks = []
            for kc in range(tk // CHUNK):
                blk = s[kc * CHUNK:(kc + 1) * CHUNK].reshape(CHUNK // SUBLANES, SUBLANES, cols)
                blocks.append(jnp.where((q_chunk >= kc)[None], blk, -jnp.inf).reshape(CHUNK, cols))
            s = jnp.concatenate(blocks, axis=0)
        s4 = s.reshape(MAX_WAYS, groups // MAX_WAYS, SUBLANES, cols)
        m_tile = jnp.max(jnp.max(jnp.max(s4, axis=1), axis=0), axis=0, keepdims=True)
        m_prev = m_ref[hh]
        m_new = jnp.maximum(m_prev, m_tile)
        alpha = jnp.exp2(m_prev - m_new)
        p = jnp.exp2(s.reshape(groups, SUBLANES, cols) - m_new[None]).reshape(tk, cols).astype(BF16)
        lhs = jnp.concatenate([vt_ref[j, ch, :], jnp.ones((ONES_ROWS, tk), BF16)], axis=0)
        pv = _dot(lhs, p)
        acc3 = acc_ref[hh].reshape(ACC_ROWS // SUBLANES, SUBLANES, cols)
        acc_ref[hh] = (alpha[None] * acc3).reshape(ACC_ROWS, cols) + pv
        m_ref[hh] = m_new

    def store_scores(j, buf):
        for hh in range(heads):
            buf[hh] = scores(hh, j)

    def full_step(j, cur, nxt):
        store_scores(j + 1, nxt)
        for hh in range(heads):
            softmax_pv(hh, j, cur[hh], masked=False)

    def last_step(j, cur):
        for hh in range(heads):
            softmax_pv(hh, j, cur[hh], masked=True)

    n_full = (i * tq) // tk
    store_scores(0, sa_ref)

    def pair(t, carry):
        full_step(2 * t, sa_ref, sb_ref)
        full_step(2 * t + 1, sb_ref, sa_ref)
        return carry

    lax.fori_loop(0, n_full // 2, pair, 0)

    @pl.when(n_full % 2 == 1)
    def _():
        full_step(n_full - 1, sa_ref, sb_ref)
        last_step(n_full, sb_ref)

    @pl.when(n_full % 2 == 0)
    def _():
        last_step(n_full, sa_ref)

    lam_vec = lam_ref[...]
    lam = (jnp.exp(jnp.sum(lam_vec[0:1] * lam_vec[1:2], axis=1, keepdims=True))
           - jnp.exp(jnp.sum(lam_vec[2:3] * lam_vec[3:4], axis=1, keepdims=True))
           + lam_init)
    for hh in range(heads):
        acc = acc_ref[hh]
        ot = acc[:V_HEAD_DIM] / acc[V_HEAD_DIM:V_HEAD_DIM + 1]
        od = (ot[:, :tq] - lam * ot[:, tq:]).T
        y = od * lax.rsqrt(jnp.mean(od * od, axis=-1, keepdims=True) + EPS)
        o_ref[:, hh * V_HEAD_DIM:(hh + 1) * V_HEAD_DIM] = (
            (y * g_ref[hh]) * (1.0 - lam_init)).astype(o_ref.dtype)


def _diff_attention(q, k, vt, lam_vec, head_g, *, batch, seq, lam_init, tq, heads):
    T, attn_width = q.shape
    tk = vt.shape[2]
    n_heads = attn_width // V_HEAD_DIM
    nq = seq // tq
    nk = seq // tk
    width = heads * V_HEAD_DIM
    kern = functools.partial(_attn_kernel, tq=tq, tk=tk, heads=heads, lam_init=lam_init)
    q_spec = pl.BlockSpec((tq, width), lambda b, h, i: (b * nq + i, h))
    return pl.pallas_call(
        kern,
        grid=(batch, n_heads // heads, nq),
        in_specs=[
            pl.BlockSpec(lam_vec.shape, lambda b, h, i: (0, 0)),
            pl.BlockSpec((heads, 1, V_HEAD_DIM), lambda b, h, i: (h, 0, 0)),
            q_spec,
            pl.BlockSpec((seq, width), lambda b, h, i: (b, h)),
            pl.BlockSpec((nk, width, tk), lambda b, h, i: (b, h, 0)),
        ],
        out_specs=q_spec,
        out_shape=jax.ShapeDtypeStruct((T, attn_width), BF16),
        scratch_shapes=[
            pltpu.VMEM((heads, SUBLANES, 2 * tq), F32),
            pltpu.VMEM((heads, ACC_ROWS, 2 * tq), F32),
            pltpu.VMEM((heads, tk, 2 * tq), F32),
            pltpu.VMEM((heads, tk, 2 * tq), F32),
        ],
        compiler_params=pltpu.CompilerParams(
            dimension_semantics=("arbitrary", "arbitrary", "arbitrary"),
            vmem_limit_bytes=32 * 1024 * 1024),
        name="diff_attention",
    )(lam_vec, head_g, q, k, vt)


def _ffn_kernel(x_ref, pool_ref, attn_ref, wo_ref, gf_ref, wup_ref, cw_ref, cb_ref, wd_ref,
                gl_ref, o_ref, carry_ref, x1_ref, h_ref, a_ref, u00_ref, u01_ref, u10_ref, u11_ref,
                *, tm, slab, chunk, tiles_per_seq, n_chunks, apply_final):
    u_refs = ((u00_ref, u01_ref), (u10_ref, u11_ref))
    seq_tile = pl.program_id(0) % tiles_per_seq

    @pl.when(seq_tile == 0)
    def _():
        carry_ref[...] = jnp.zeros_like(carry_ref)

    mixed = jnp.concatenate([pool_ref[...], attn_ref[...]], axis=1)
    x1 = x_ref[...] + _dot(mixed, wo_ref[...])
    x1_ref[...] = x1
    ms = jnp.mean(x1 * x1, axis=-1, keepdims=True)
    h_ref[...] = ((x1 * lax.rsqrt(ms + EPS)) * gf_ref[...]).astype(BF16)

    n_slabs = tm // slab
    cols = lambda idx: slice(idx * chunk, (idx + 1) * chunk)

    def up_matmul(c, r):
        h = h_ref[r * slab:(r + 1) * slab, :]
        return [_dot(h, wup_ref[:, cols(idx)]) for idx in (c, n_chunks + c)]

    def up_store(c, slot, r, us):
        for br, idx in enumerate((c, n_chunks + c)):
            u_ref = u_refs[slot][br]
            if r == 0:
                u_ref[0:CONV_HALO, :] = carry_ref[:, cols(idx)]
            u_ref[CONV_HALO + r * slab:CONV_HALO + (r + 1) * slab, :] = us[br]
            if r == n_slabs - 1:
                carry_ref[:, cols(idx)] = u_ref[tm:, :]

    def conv_branch(slot, br, idx, r):
        r0 = CONV_HALO + r * slab
        u_ref = u_refs[slot][br]
        u = u_ref[r0:r0 + slab, :]
        u1 = u_ref[r0 - 1:r0 - 1 + slab, :]
        u2 = u_ref[r0 - 2:r0 - 2 + slab, :]
        cw = cw_ref[:, cols(idx)]
        return ((cb_ref[:, cols(idx)] + cw[0:1] * u2) + cw[1:2] * u1) + cw[2:3] * u

    for r in range(n_slabs):
        up_store(0, 0, r, up_matmul(0, r))
    for c in range(n_chunks):
        slot = c % 2
        for r in range(n_slabs):
            nxt = up_matmul(c + 1, r) if c + 1 < n_chunks else None
            gate = conv_branch(slot, 0, c, r)
            val = conv_branch(slot, 1, n_chunks + c, r)
            act = (gate * jax.nn.sigmoid(gate)) * val
            a_ref[r * slab:(r + 1) * slab, cols(c)] = act.astype(BF16)
            if nxt is not None:
                up_store(c + 1, 1 - slot, r, nxt)
    ffn_out = _dot(a_ref[...], wd_ref[...])

    x2 = x1_ref[...] + ffn_out
    if apply_final:
        ms2 = jnp.mean(x2 * x2, axis=-1, keepdims=True)
        x2 = (x2 * lax.rsqrt(ms2 + EPS)) * gl_ref[...]
    o_ref[...] = x2


def _ffn(x2d, pool_out, attn_out, wo, g_ffn, wup, cw, cb, wd, g_final,
         *, seq, tm, chunk, apply_final):
    T, d_model = x2d.shape
    n_chunks = wd.shape[0] // chunk
    kern = functools.partial(_ffn_kernel, tm=tm, tiles_per_seq=seq // tm,
                             slab=MXU_DIM, chunk=chunk, n_chunks=n_chunks, apply_final=apply_final)
    row = lambda i: (i, 0)
    const2 = lambda i: (0, 0)
    resident = dict(pipeline_mode=pl.Buffered(1))
    return pl.pallas_call(
        kern,
        grid=(T // tm,),
        in_specs=[
            pl.BlockSpec((tm, d_model), row),
            pl.BlockSpec((tm, pool_out.shape[1]), row),
            pl.BlockSpec((tm, attn_out.shape[1]), row),
            pl.BlockSpec(wo.shape, const2, **resident),
            pl.BlockSpec((1, d_model), const2),
            pl.BlockSpec(wup.shape, const2, **resident),
            pl.BlockSpec(cw.shape, const2),
            pl.BlockSpec(cb.shape, const2),
            pl.BlockSpec(wd.shape, const2, **resident),
            pl.BlockSpec((1, d_model), const2),
        ],
        out_specs=pl.BlockSpec((tm, d_model), row),
        out_shape=jax.ShapeDtypeStruct((T, d_model), F32),
        scratch_shapes=[
            pltpu.VMEM((CONV_HALO, 2 * n_chunks * chunk), F32),
            pltpu.VMEM((tm, d_model), F32),
            pltpu.VMEM((tm, d_model), BF16),
            pltpu.VMEM((tm, n_chunks * chunk), BF16),
        ] + [pltpu.VMEM((CONV_HALO + tm, chunk), F32)] * 4,
        compiler_params=pltpu.CompilerParams(
            dimension_semantics=("arbitrary",),
            vmem_limit_bytes=56 * 1024 * 1024),
        name="outproj_convffn",
    )(x2d, pool_out, attn_out, wo, g_ffn, wup, cw, cb, wd, g_final)


def kernel(x, positions, norm_mix_g, w_in, pool_w, pool_scale, lambda_q1, lambda_k1, lambda_q2,
           lambda_k2, attn_norm_g, w_o, norm_ffn_g, w_up, conv_w, conv_b, w_down, norm_final_g):
    batch, seq, d_model = x.shape
    depth = w_in.shape[0]
    T = batch * seq
    d_ff = w_down.shape[1]
    n_heads = attn_norm_g.shape[1]
    chunk = MXU_DIM
    assert d_ff % chunk == 0 and seq % 512 == 0

    x2d = x.reshape(T, d_model)
    tm_in = 512
    half = DIFF_HEAD_DIM // 2
    groups = LANES // half
    pos_c = positions.astype(F32).reshape(T // tm_in, groups, tm_in // groups).transpose(0, 2, 1)
    pos_c = jnp.broadcast_to(pos_c[..., None], pos_c.shape + (half,)).reshape(T // groups, LANES)
    inv_freq = ROPE_THETA ** (-jnp.arange(0, DIFF_HEAD_DIM, 2, dtype=F32) / DIFF_HEAD_DIM)
    invf = jnp.tile(inv_freq, groups).reshape(1, LANES)

    for l in range(depth):
        lam_init = 0.8 - 0.6 * float(np.exp(-0.3 * l))
        pool_out, q, k, vt = _inproj(
            x2d, pos_c, invf, norm_mix_g[l].reshape(1, d_model), w_in[l].astype(BF16),
            pool_w[l].astype(BF16), pool_scale[l].reshape(1, -1), seq=seq, tm=tm_in)
        lam_vec = jnp.stack([lambda_q1[l], lambda_k1[l], lambda_q2[l], lambda_k2[l]]).astype(F32)
        attn_out = _diff_attention(
            q, k, vt, lam_vec, attn_norm_g[l].reshape(n_heads, 1, V_HEAD_DIM),
            batch=batch, seq=seq, lam_init=lam_init, tq=256, heads=4)
        x2d = _ffn(
            x2d, pool_out, attn_out, w_o[l].astype(BF16), norm_ffn_g[l].reshape(1, d_model),
            w_up[l].astype(BF16), conv_w[l], conv_b[l].reshape(1, -1), w_down[l].astype(BF16),
            norm_final_g.reshape(1, d_model),
            seq=seq, tm=512, chunk=chunk, apply_final=(l == depth - 1))
    return x2d.reshape(batch, seq, d_model)
```

```python
import functools

import numpy as np
import jax
import jax.numpy as jnp
from jax import lax
from jax.experimental import pallas as pl
from jax.experimental.pallas import tpu as pltpu

CHUNK = 64
CHUNK_LOG2 = CHUNK.bit_length() - 1
POOL_WINDOWS = (2, 4, 8, 16)
POOL_GROUP_DIM = 128
DIFF_HEAD_DIM = 64
V_HEAD_DIM = 2 * DIFF_HEAD_DIM
CONV_WIDTH = 3
ROPE_THETA = 10000.0
EPS = 1e-6
LOG2_E = 1.4426950408889634

LANES = 128
SUBLANES = 8
MXU_DIM = 256

POOL_HALO = 16
CONV_HALO = SUBLANES
ONES_ROWS = 16
ACC_ROWS = V_HEAD_DIM + ONES_ROWS
MAX_WAYS = 8

BF16 = jnp.bfloat16
F32 = jnp.float32


def _dot(a, b):
    return jnp.dot(a, b, preferred_element_type=F32)


def _shift_rows(a, k):
    return pltpu.roll(a, k, axis=0)


def _inproj_kernel(x_ref, pos_ref, invf_ref, g_ref, w_ref, pw_ref, ps_ref,
                   pool_ref, q_ref, k_ref, v_ref, carry_ref, *, tm, tiles_per_seq,
                   pool_width, attn_width):
    seq_tile = pl.program_id(0) % tiles_per_seq

    @pl.when(seq_tile == 0)
    def _():
        carry_ref[...] = jnp.zeros_like(carry_ref)

    x = x_ref[...]
    ms = jnp.mean(x * x, axis=-1, keepdims=True)
    h = ((x * lax.rsqrt(ms + EPS)) * g_ref[...]).astype(BF16)

    p = _dot(h, w_ref[:, 0:pool_width])
    t = seq_tile * tm + lax.broadcasted_iota(jnp.int32, (tm, POOL_GROUP_DIM), 0)
    pooled_out = []
    for g, w in enumerate(POOL_WINDOWS):
        cols = slice(g * POOL_GROUP_DIM, (g + 1) * POOL_GROUP_DIM)
        pg = p[:, cols]
        s = jnp.concatenate([carry_ref[:, cols], pg], axis=0)
        span = 1
        while span < w:
            s = s + _shift_rows(s, span)
            span *= 2
        count = jnp.minimum(t + 1, w).astype(F32)
        d = (s[POOL_HALO:] / count - pg).astype(BF16)
        pooled_out.append(_dot(d, pw_ref[g]))
    carry_ref[...] = p[tm - POOL_HALO:, :]
    pool_ref[...] = (jnp.concatenate(pooled_out, axis=1) * ps_ref[...]).astype(pool_ref.dtype)

    half = DIFF_HEAD_DIM // 2
    groups = LANES // half
    tq = tm // groups
    ang = pos_ref[...] * invf_ref[...]
    cos_c = jnp.cos(ang)
    sin_c = jnp.sin(ang)
    lane = lax.broadcasted_iota(jnp.int32, (tq, LANES), 1)
    lane_group = lane // half
    first_half = (lane % DIFF_HEAD_DIM) < half
    sign = jnp.where(first_half, -1.0, 1.0).astype(F32)

    def spread(x, a):
        y = None
        for g in range(groups):
            shift = ((g - a) * half) % LANES
            xr = x if shift == 0 else pltpu.roll(x, shift, axis=1)
            y = xr if y is None else jnp.where(lane_group == g, xr, y)
        return y

    tables = [(spread(cos_c, a), spread(sin_c, a) * sign) for a in range(groups)]

    def rope(tt, scale):
        blocks = []
        for a, (cos, sin) in enumerate(tables):
            outs = []
            for j in range(attn_width // LANES):
                tj = tt[a * tq:(a + 1) * tq, j * LANES:(j + 1) * LANES]
                partner = jnp.where(first_half,
                                    pltpu.roll(tj, LANES - half, axis=1),
                                    pltpu.roll(tj, half, axis=1))
                outs.append((tj * cos + partner * sin) * scale)
            blocks.append(jnp.concatenate(outs, axis=1))
        return jnp.concatenate(blocks, axis=0)

    q0 = pool_width
    k0 = pool_width + attn_width
    v0 = pool_width + 2 * attn_width
    q = _dot(h, w_ref[:, q0:k0])
    q_ref[...] = rope(q, DIFF_HEAD_DIM ** -0.5 * LOG2_E).astype(q_ref.dtype)
    k = _dot(h, w_ref[:, k0:v0])
    k_ref[...] = rope(k, 1.0).astype(k_ref.dtype)
    v_ref[0] = _dot(h, w_ref[:, v0:v0 + attn_width]).T.astype(v_ref.dtype)


def _inproj(x2, pos_b, invf, g, w, pool_w, pool_scale, *, seq, tm):
    T, d_model = x2.shape
    pool_width = pool_scale.shape[-1]
    attn_width = (w.shape[1] - pool_width) // 3
    n_groups = pool_w.shape[0]
    kern = functools.partial(_inproj_kernel, tm=tm, tiles_per_seq=seq // tm,
                             pool_width=pool_width, attn_width=attn_width)
    row = lambda i: (i, 0)
    const2 = lambda i: (0, 0)
    out_sds = lambda width: jax.ShapeDtypeStruct((T, width), BF16)
    return pl.pallas_call(
        kern,
        grid=(T // tm,),
        in_specs=[
            pl.BlockSpec((tm, d_model), row),
            pl.BlockSpec((tm * (DIFF_HEAD_DIM // 2) // LANES, LANES), row),
            pl.BlockSpec((1, LANES), const2),
            pl.BlockSpec((1, d_model), const2),
            pl.BlockSpec(w.shape, const2),
            pl.BlockSpec(pool_w.shape, lambda i: (0, 0, 0)),
            pl.BlockSpec((1, pool_width), const2),
        ],
        out_specs=[
            pl.BlockSpec((tm, pool_width), row),
            pl.BlockSpec((tm, attn_width), row),
            pl.BlockSpec((tm, attn_width), row),
            pl.BlockSpec((1, attn_width, tm), lambda i: (i, 0, 0)),
        ],
        out_shape=[out_sds(pool_width), out_sds(attn_width), out_sds(attn_width),
                   jax.ShapeDtypeStruct((T // tm, attn_width, tm), BF16)],
        scratch_shapes=[pltpu.VMEM((POOL_HALO, n_groups * POOL_GROUP_DIM), F32)],
        compiler_params=pltpu.CompilerParams(
            dimension_semantics=("arbitrary",),
            vmem_limit_bytes=48 * 1024 * 1024),
        name="inproj_pool_rope",
    )(x2, pos_b, invf, g, w, pool_w, pool_scale)


def _attn_kernel(lam_ref, g_ref, q_ref, k_ref, vt_ref, o_ref, m_ref, acc_ref, sa_ref, sb_ref,
                 *, tq, tk, heads, lam_init):
    i = pl.program_id(2)
    cols = 2 * tq
    groups = tk // SUBLANES

    sub = lax.broadcasted_iota(jnp.int32, (V_HEAD_DIM, tq), 0)
    qqts = []
    for hh in range(heads):
        qt = q_ref[:, hh * V_HEAD_DIM:(hh + 1) * V_HEAD_DIM].astype(F32).T
        zero = jnp.zeros_like(qt)
        qqts.append(jnp.concatenate([jnp.where(sub < DIFF_HEAD_DIM, qt, zero),
                                     jnp.where(sub >= DIFF_HEAD_DIM, qt, zero)],
                                    axis=1).astype(BF16))

    m_ref[...] = jnp.full(m_ref.shape, -jnp.inf, F32)
    acc_ref[...] = jnp.zeros(acc_ref.shape, F32)

    def scores(hh, j):
        ch = slice(hh * V_HEAD_DIM, (hh + 1) * V_HEAD_DIM)
        kt = k_ref[pl.ds(pl.multiple_of(j * tk, tk), tk), ch]
        return _dot(kt, qqts[hh])

    def softmax_pv(hh, j, s, masked):
        ch = slice(hh * V_HEAD_DIM, (hh + 1) * V_HEAD_DIM)
        if masked:
            c = lax.broadcasted_iota(jnp.int32, (SUBLANES, cols), 1)
            q_chunk = (i * tq - j * tk + jnp.where(c >= tq, c - tq, c)) >> CHUNK_LOG2
            blocks = []
            for kc in range(tk // CHUNK):
                blk = s[kc * CHUNK:(kc + 1) * CHUNK].reshape(CHUNK // SUBLANES, SUBLANES, cols)
                blocks.append(jnp.where((q_chunk >= kc)[None], blk, -jnp.inf).reshape(CHUNK, cols))
            s = jnp.concatenate(blocks, axis=0)
        s4 = s.reshape(MAX_WAYS, groups // MAX_WAYS, SUBLANES, cols)
        m_tile = jnp.max(jnp.max(jnp.max(s4, axis=1), axis=0), axis=0, keepdims=True)
        m_prev = m_ref[hh]
        m_new = jnp.maximum(m_prev, m_tile)
        alpha = jnp.exp2(m_prev - m_new)
        p = jnp.exp2(s.reshape(groups, SUBLANES, cols) - m_new[None]).reshape(tk, cols).astype(BF16)
        lhs = jnp.concatenate([vt_ref[j, ch, :], jnp.ones((ONES_ROWS, tk), BF16)], axis=0)
        pv = _dot(lhs, p)
        acc3 = acc_ref[hh].reshape(ACC_ROWS // SUBLANES, SUBLANES, cols)
        acc_ref[hh] = (alpha[None] * acc3).reshape(ACC_ROWS, cols) + pv
        m_ref[hh] = m_new

    def store_scores(j, buf):
        for hh in range(heads):
            buf[hh] = scores(hh, j)

    def full_step(j, cur, nxt):
        store_scores(j + 1, nxt)
        for hh in range(heads):
            softmax_pv(hh, j, cur[hh], masked=False)

    def last_step(j, cur):
        for hh in range(heads):
            softmax_pv(hh, j, cur[hh], masked=True)

    n_full = (i * tq) // tk
    store_scores(0, sa_ref)

    def pair(t, carry):
        full_step(2 * t, sa_ref, sb_ref)
        full_step(2 * t + 1, sb_ref, sa_ref)
        return carry

    lax.fori_loop(0, n_full // 2, pair, 0)

    @pl.when(n_full % 2 == 1)
    def _():
        full_step(n_full - 1, sa_ref, sb_ref)
        last_step(n_full, sb_ref)

    @pl.when(n_full % 2 == 0)
    def _():
        last_step(n_full, sa_ref)

    lam_vec = lam_ref[...]
    lam = (jnp.exp(jnp.sum(lam_vec[0:1] * lam_vec[1:2], axis=1, keepdims=True))
           - jnp.exp(jnp.sum(lam_vec[2:3] * lam_vec[3:4], axis=1, keepdims=True))
           + lam_init)
    for hh in range(heads):
        acc = acc_ref[hh]
        ot = acc[:V_HEAD_DIM] / acc[V_HEAD_DIM:V_HEAD_DIM + 1]
        od = (ot[:, :tq] - lam * ot[:, tq:]).T
        y = od * lax.rsqrt(jnp.mean(od * od, axis=-1, keepdims=True) + EPS)
        o_ref[:, hh * V_HEAD_DIM:(hh + 1) * V_HEAD_DIM] = (
            (y * g_ref[hh]) * (1.0 - lam_init)).astype(o_ref.dtype)


def _diff_attention(q, k, vt, lam_vec, head_g, *, batch, seq, lam_init, tq, heads):
    T, attn_width = q.shape
    tk = vt.shape[2]
    n_heads = attn_width // V_HEAD_DIM
    nq = seq // tq
    nk = seq // tk
    width = heads * V_HEAD_DIM
    kern = functools.partial(_attn_kernel, tq=tq, tk=tk, heads=heads, lam_init=lam_init)
    q_spec = pl.BlockSpec((tq, width), lambda b, h, i: (b * nq + i, h))
    return pl.pallas_call(
        kern,
        grid=(batch, n_heads // heads, nq),
        in_specs=[
            pl.BlockSpec(lam_vec.shape, lambda b, h, i: (0, 0)),
            pl.BlockSpec((heads, 1, V_HEAD_DIM), lambda b, h, i: (h, 0, 0)),
            q_spec,
            pl.BlockSpec((seq, width), lambda b, h, i: (b, h)),
            pl.BlockSpec((nk, width, tk), lambda b, h, i: (b, h, 0)),
        ],
        out_specs=q_spec,
        out_shape=jax.ShapeDtypeStruct((T, attn_width), BF16),
        scratch_shapes=[
            pltpu.VMEM((heads, SUBLANES, 2 * tq), F32),
            pltpu.VMEM((heads, ACC_ROWS, 2 * tq), F32),
            pltpu.VMEM((heads, tk, 2 * tq), F32),
            pltpu.VMEM((heads, tk, 2 * tq), F32),
        ],
        compiler_params=pltpu.CompilerParams(
            dimension_semantics=("arbitrary", "arbitrary", "arbitrary"),
            vmem_limit_bytes=52 * 1024 * 1024),
        name="diff_attention",
    )(lam_vec, head_g, q, k, vt)


def _ffn_kernel(x_ref, pool_ref, attn_ref, wo_ref, gf_ref, wup_ref, cw_ref, cb_ref, wd_ref,
                gl_ref, o_ref, carry_ref, x1_ref, h_ref, a_ref, u00_ref, u01_ref, u10_ref, u11_ref,
                *, tm, slab, chunk, tiles_per_seq, n_chunks, apply_final):
    u_refs = ((u00_ref, u01_ref), (u10_ref, u11_ref))
    seq_tile = pl.program_id(0) % tiles_per_seq

    @pl.when(seq_tile == 0)
    def _():
        carry_ref[...] = jnp.zeros_like(carry_ref)

    mixed = jnp.concatenate([pool_ref[...], attn_ref[...]], axis=1)
    x1 = x_ref[...] + _dot(mixed, wo_ref[...])
    x1_ref[...] = x1
    ms = jnp.mean(x1 * x1, axis=-1, keepdims=True)
    h_ref[...] = ((x1 * lax.rsqrt(ms + EPS)) * gf_ref[...]).astype(BF16)

    n_slabs = tm // slab
    cols = lambda idx: slice(idx * chunk, (idx + 1) * chunk)

    def up_matmul(c, r):
        h = h_ref[r * slab:(r + 1) * slab, :]
        return [_dot(h, wup_ref[:, cols(idx)]) for idx in (c, n_chunks + c)]

    def up_store(c, slot, r, us):
        for br, idx in enumerate((c, n_chunks + c)):
            u_ref = u_refs[slot][br]
            if r == 0:
                u_ref[0:CONV_HALO, :] = carry_ref[:, cols(idx)]
            u_ref[CONV_HALO + r * slab:CONV_HALO + (r + 1) * slab, :] = us[br]
            if r == n_slabs - 1:
                carry_ref[:, cols(idx)] = u_ref[tm:, :]

    def conv_branch(slot, br, idx, r):
        r0 = CONV_HALO + r * slab
        u_ref = u_refs[slot][br]
        u = u_ref[r0:r0 + slab, :]
        u1 = u_ref[r0 - 1:r0 - 1 + slab, :]
        u2 = u_ref[r0 - 2:r0 - 2 + slab, :]
        cw = cw_ref[:, cols(idx)]
        return ((cb_ref[:, cols(idx)] + cw[0:1] * u2) + cw[1:2] * u1) + cw[2:3] * u

    for r in range(n_slabs):
        up_store(0, 0, r, up_matmul(0, r))
    for c in range(n_chunks):
        slot = c % 2
        for r in range(n_slabs):
            nxt = up_matmul(c + 1, r) if c + 1 < n_chunks else None
            gate = conv_branch(slot, 0, c, r)
            val = conv_branch(slot, 1, n_chunks + c, r)
            act = (gate * jax.nn.sigmoid(gate)) * val
            a_ref[r * slab:(r + 1) * slab, cols(c)] = act.astype(BF16)
            if nxt is not None:
                up_store(c + 1, 1 - slot, r, nxt)
    ffn_out = _dot(a_ref[...], wd_ref[...])

    x2 = x1_ref[...] + ffn_out
    if apply_final:
        ms2 = jnp.mean(x2 * x2, axis=-1, keepdims=True)
        x2 = (x2 * lax.rsqrt(ms2 + EPS)) * gl_ref[...]
    o_ref[...] = x2


def _ffn(x2d, pool_out, attn_out, wo, g_ffn, wup, cw, cb, wd, g_final,
         *, seq, tm, chunk, apply_final):
    T, d_model = x2d.shape
    n_chunks = wd.shape[0] // chunk
    kern = functools.partial(_ffn_kernel, tm=tm, tiles_per_seq=seq // tm,
                             slab=MXU_DIM, chunk=chunk, n_chunks=n_chunks, apply_final=apply_final)
    row = lambda i: (i, 0)
    const2 = lambda i: (0, 0)
    resident = dict(pipeline_mode=pl.Buffered(1))
    return pl.pallas_call(
        kern,
        grid=(T // tm,),
        in_specs=[
            pl.BlockSpec((tm, d_model), row),
            pl.BlockSpec((tm, pool_out.shape[1]), row),
            pl.BlockSpec((tm, attn_out.shape[1]), row),
            pl.BlockSpec(wo.shape, const2, **resident),
            pl.BlockSpec((1, d_model), const2),
            pl.BlockSpec(wup.shape, const2, **resident),
            pl.BlockSpec(cw.shape, const2),
            pl.BlockSpec(cb.shape, const2),
            pl.BlockSpec(wd.shape, const2, **resident),
            pl.BlockSpec((1, d_model), const2),
        ],
        out_specs=pl.BlockSpec((tm, d_model), row),
        out_shape=jax.ShapeDtypeStruct((T, d_model), F32),
        scratch_shapes=[
            pltpu.VMEM((CONV_HALO, 2 * n_chunks * chunk), F32),
            pltpu.VMEM((tm, d_model), F32),
            pltpu.VMEM((tm, d_model), BF16),
            pltpu.VMEM((tm, n_chunks * chunk), BF16),
        ] + [pltpu.VMEM((CONV_HALO + tm, chunk), F32)] * 4,
        compiler_params=pltpu.CompilerParams(
            dimension_semantics=("arbitrary",),
            vmem_limit_bytes=56 * 1024 * 1024),
        name="outproj_convffn",
    )(x2d, pool_out, attn_out, wo, g_ffn, wup, cw, cb, wd, g_final)


def kernel(x, positions, norm_mix_g, w_in, pool_w, pool_scale, lambda_q1, lambda_k1, lambda_q2,
           lambda_k2, attn_norm_g, w_o, norm_ffn_g, w_up, conv_w, conv_b, w_down, norm_final_g):
    batch, seq, d_model = x.shape
    depth = w_in.shape[0]
    T = batch * seq
    d_ff = w_down.shape[1]
    n_heads = attn_norm_g.shape[1]
    chunk = MXU_DIM
    assert d_ff % chunk == 0 and seq % 512 == 0

    x2d = x.reshape(T, d_model)
    tm_in = 512
    half = DIFF_HEAD_DIM // 2
    groups = LANES // half
    pos_c = positions.astype(F32).reshape(T // tm_in, groups, tm_in // groups).transpose(0, 2, 1)
    pos_c = jnp.broadcast_to(pos_c[..., None], pos_c.shape + (half,)).reshape(T // groups, LANES)
    inv_freq = ROPE_THETA ** (-jnp.arange(0, DIFF_HEAD_DIM, 2, dtype=F32) / DIFF_HEAD_DIM)
    invf = jnp.tile(inv_freq, groups).reshape(1, LANES)

    for l in range(depth):
        lam_init = 0.8 - 0.6 * float(np.exp(-0.3 * l))
        pool_out, q, k, vt = _inproj(
            x2d, pos_c, invf, norm_mix_g[l].reshape(1, d_model), w_in[l].astype(BF16),
            pool_w[l].astype(BF16), pool_scale[l].reshape(1, -1), seq=seq, tm=tm_in)
        lam_vec = jnp.stack([lambda_q1[l], lambda_k1[l], lambda_q2[l], lambda_k2[l]]).astype(F32)
        attn_out = _diff_attention(
            q, k, vt, lam_vec, attn_norm_g[l].reshape(n_heads, 1, V_HEAD_DIM),
            batch=batch, seq=seq, lam_init=lam_init, tq=512, heads=4)
        x2d = _ffn(
            x2d, pool_out, attn_out, w_o[l].astype(BF16), norm_ffn_g[l].reshape(1, d_model),
            w_up[l].astype(BF16), conv_w[l], conv_b[l].reshape(1, -1), w_down[l].astype(BF16),
            norm_final_g.reshape(1, d_model),
            seq=seq, tm=512, chunk=chunk, apply_final=(l == depth - 1))
    return x2d.reshape(batch, seq, d_model)
```

```python
import functools

import numpy as np
import jax
import jax.numpy as jnp
from jax import lax
from jax.experimental import pallas as pl
from jax.experimental.pallas import tpu as pltpu

CHUNK = 64
CHUNK_LOG2 = CHUNK.bit_length() - 1
POOL_WINDOWS = (2, 4, 8, 16)
POOL_GROUP_DIM = 128
DIFF_HEAD_DIM = 64
V_HEAD_DIM = 2 * DIFF_HEAD_DIM
CONV_WIDTH = 3
ROPE_THETA = 10000.0
EPS = 1e-6
LOG2_E = 1.4426950408889634

LANES = 128
SUBLANES = 8
MXU_DIM = 256

POOL_HALO = 16
CONV_HALO = SUBLANES
ONES_ROWS = 16
ACC_ROWS = V_HEAD_DIM + ONES_ROWS
MAX_WAYS = 8

BF16 = jnp.bfloat16
F32 = jnp.float32


def _dot(a, b):
    return jnp.dot(a, b, preferred_element_type=F32)


def _shift_rows(a, k):
    return pltpu.roll(a, k, axis=0)


def _inproj_kernel(x_ref, pos_ref, invf_ref, g_ref, w_ref, pw_ref, ps_ref,
                   pool_ref, q_ref, k_ref, v_ref, carry_ref, *, tm, tiles_per_seq,
                   pool_width, attn_width):
    seq_tile = pl.program_id(0) % tiles_per_seq

    @pl.when(seq_tile == 0)
    def _():
        carry_ref[...] = jnp.zeros_like(carry_ref)

    x = x_ref[...]
    ms = jnp.mean(x * x, axis=-1, keepdims=True)
    h = ((x * lax.rsqrt(ms + EPS)) * g_ref[...]).astype(BF16)

    q0 = pool_width
    k0 = pool_width + attn_width
    v0 = pool_width + 2 * attn_width
    p = _dot(h, w_ref[:, 0:pool_width])
    q = _dot(h, w_ref[:, q0:k0])
    k = _dot(h, w_ref[:, k0:v0])
    v = _dot(h, w_ref[:, v0:v0 + attn_width])

    t = seq_tile * tm + lax.broadcasted_iota(jnp.int32, (tm, POOL_GROUP_DIM), 0)
    pooled_out = []
    for g, w in enumerate(POOL_WINDOWS):
        cols = slice(g * POOL_GROUP_DIM, (g + 1) * POOL_GROUP_DIM)
        pg = p[:, cols]
        s = jnp.concatenate([carry_ref[:, cols], pg], axis=0)
        span = 1
        while span < w:
            s = s + _shift_rows(s, span)
            span *= 2
        count = jnp.minimum(t + 1, w).astype(F32)
        d = (s[POOL_HALO:] / count - pg).astype(BF16)
        pooled_out.append(_dot(d, pw_ref[g]))
    carry_ref[...] = p[tm - POOL_HALO:, :]
    pool_ref[...] = (jnp.concatenate(pooled_out, axis=1) * ps_ref[...]).astype(pool_ref.dtype)

    half = DIFF_HEAD_DIM // 2
    groups = LANES // half
    tq = tm // groups
    ang = pos_ref[...] * invf_ref[...]
    cos_c = jnp.cos(ang)
    sin_c = jnp.sin(ang)
    lane = lax.broadcasted_iota(jnp.int32, (tq, LANES), 1)
    lane_group = lane // half
    first_half = (lane % DIFF_HEAD_DIM) < half
    sign = jnp.where(first_half, -1.0, 1.0).astype(F32)

    def spread(x, a):
        y = None
        for g in range(groups):
            shift = ((g - a) * half) % LANES
            xr = x if shift == 0 else pltpu.roll(x, shift, axis=1)
            y = xr if y is None else jnp.where(lane_group == g, xr, y)
        return y

    tables = [(spread(cos_c, a), spread(sin_c, a) * sign) for a in range(groups)]

    def rope(tt, scale):
        blocks = []
        for a, (cos, sin) in enumerate(tables):
            outs = []
            for j in range(attn_width // LANES):
                tj = tt[a * tq:(a + 1) * tq, j * LANES:(j + 1) * LANES]
                partner = jnp.where(first_half,
                                    pltpu.roll(tj, LANES - half, axis=1),
                                    pltpu.roll(tj, half, axis=1))
                outs.append((tj * cos + partner * sin) * scale)
            blocks.append(jnp.concatenate(outs, axis=1))
        return jnp.concatenate(blocks, axis=0)

    q_ref[...] = rope(q, DIFF_HEAD_DIM ** -0.5 * LOG2_E).astype(q_ref.dtype)
    k_ref[...] = rope(k, 1.0).astype(k_ref.dtype)
    v_ref[0] = v.T.astype(v_ref.dtype)


def _inproj(x2, pos_b, invf, g, w, pool_w, pool_scale, *, seq, tm):
    T, d_model = x2.shape
    pool_width = pool_scale.shape[-1]
    attn_width = (w.shape[1] - pool_width) // 3
    n_groups = pool_w.shape[0]
    kern = functools.partial(_inproj_kernel, tm=tm, tiles_per_seq=seq // tm,
                             pool_width=pool_width, attn_width=attn_width)
    row = lambda i: (i, 0)
    const2 = lambda i: (0, 0)
    out_sds = lambda width: jax.ShapeDtypeStruct((T, width), BF16)
    return pl.pallas_call(
        kern,
        grid=(T // tm,),
        in_specs=[
            pl.BlockSpec((tm, d_model), row),
            pl.BlockSpec((tm * (DIFF_HEAD_DIM // 2) // LANES, LANES), row),
            pl.BlockSpec((1, LANES), const2),
            pl.BlockSpec((1, d_model), const2),
            pl.BlockSpec(w.shape, const2),
            pl.BlockSpec(pool_w.shape, lambda i: (0, 0, 0)),
            pl.BlockSpec((1, pool_width), const2),
        ],
        out_specs=[
            pl.BlockSpec((tm, pool_width), row),
            pl.BlockSpec((tm, attn_width), row),
            pl.BlockSpec((tm, attn_width), row),
            pl.BlockSpec((1, attn_width, tm), lambda i: (i, 0, 0)),
        ],
        out_shape=[out_sds(pool_width), out_sds(attn_width), out_sds(attn_width),
                   jax.ShapeDtypeStruct((T // tm, attn_width, tm), BF16)],
        scratch_shapes=[pltpu.VMEM((POOL_HALO, n_groups * POOL_GROUP_DIM), F32)],
        compiler_params=pltpu.CompilerParams(
            dimension_semantics=("arbitrary",),
            vmem_limit_bytes=48 * 1024 * 1024),
        name="inproj_pool_rope",
    )(x2, pos_b, invf, g, w, pool_w, pool_scale)


def _attn_kernel(lam_ref, g_ref, q_ref, k_ref, vt_ref, o_ref, m_ref, acc_ref, sa_ref, sb_ref,
                 *, tq, tk, heads, lam_init):
    i = pl.program_id(2)
    cols = 2 * tq
    groups = tk // SUBLANES

    m_ref[...] = jnp.full(m_ref.shape, -jnp.inf, F32)
    acc_ref[...] = jnp.zeros(acc_ref.shape, F32)

    qqts = []

    def scores(hh, j):
        ch = slice(hh * V_HEAD_DIM, (hh + 1) * V_HEAD_DIM)
        kt = k_ref[pl.ds(pl.multiple_of(j * tk, tk), tk), ch]
        return _dot(kt, qqts[hh])

    sub = lax.broadcasted_iota(jnp.int32, (V_HEAD_DIM, tq), 0)
    for hh in range(heads):
        qt = q_ref[:, hh * V_HEAD_DIM:(hh + 1) * V_HEAD_DIM].astype(F32).T
        zero = jnp.zeros_like(qt)
        qqts.append(jnp.concatenate([jnp.where(sub < DIFF_HEAD_DIM, qt, zero),
                                     jnp.where(sub >= DIFF_HEAD_DIM, qt, zero)],
                                    axis=1).astype(BF16))
        sa_ref[hh] = scores(hh, 0)

    def softmax_pv(hh, j, s, masked):
        ch = slice(hh * V_HEAD_DIM, (hh + 1) * V_HEAD_DIM)
        if masked:
            c = lax.broadcasted_iota(jnp.int32, (SUBLANES, cols), 1)
            q_chunk = (i * tq - j * tk + jnp.where(c >= tq, c - tq, c)) >> CHUNK_LOG2
            blocks = []
            for kc in range(tk // CHUNK):
                blk = s[kc * CHUNK:(kc + 1) * CHUNK].reshape(CHUNK // SUBLANES, SUBLANES, cols)
                blocks.append(jnp.where((q_chunk >= kc)[None], blk, -jnp.inf).reshape(CHUNK, cols))
            s = jnp.concatenate(blocks, axis=0)
        s4 = s.reshape(MAX_WAYS, groups // MAX_WAYS, SUBLANES, cols)
        m_tile = jnp.max(jnp.max(jnp.max(s4, axis=1), axis=0), axis=0, keepdims=True)
        m_prev = m_ref[hh]
        m_new = jnp.maximum(m_prev, m_tile)
        alpha = jnp.exp2(m_prev - m_new)
        p = jnp.exp2(s.reshape(groups, SUBLANES, cols) - m_new[None]).reshape(tk, cols).astype(BF16)
        lhs = jnp.concatenate([vt_ref[j, ch, :], jnp.ones((ONES_ROWS, tk), BF16)], axis=0)
        pv = _dot(lhs, p)
        acc3 = acc_ref[hh].reshape(ACC_ROWS // SUBLANES, SUBLANES, cols)
        acc_ref[hh] = (alpha[None] * acc3).reshape(ACC_ROWS, cols) + pv
        m_ref[hh] = m_new

    def store_scores(j, buf):
        for hh in range(heads):
            buf[hh] = scores(hh, j)

    def full_step(j, cur, nxt):
        store_scores(j + 1, nxt)
        for hh in range(heads):
            softmax_pv(hh, j, cur[hh], masked=False)

    def last_step(j, cur):
        for hh in range(heads):
            softmax_pv(hh, j, cur[hh], masked=True)

    n_full = (i * tq) // tk

    def pair(t, carry):
        full_step(2 * t, sa_ref, sb_ref)
        full_step(2 * t + 1, sb_ref, sa_ref)
        return carry

    lax.fori_loop(0, n_full // 2, pair, 0)

    @pl.when(n_full % 2 == 1)
    def _():
        full_step(n_full - 1, sa_ref, sb_ref)
        last_step(n_full, sb_ref)

    @pl.when(n_full % 2 == 0)
    def _():
        last_step(n_full, sa_ref)

    lam_vec = lam_ref[...]
    lam = (jnp.exp(jnp.sum(lam_vec[0:1] * lam_vec[1:2], axis=1, keepdims=True))
           - jnp.exp(jnp.sum(lam_vec[2:3] * lam_vec[3:4], axis=1, keepdims=True))
           + lam_init)
    for hh in range(heads):
        acc = acc_ref[hh]
        ot = acc[:V_HEAD_DIM] / acc[V_HEAD_DIM:V_HEAD_DIM + 1]
        od = (ot[:, :tq] - lam * ot[:, tq:]).T
        y = od * lax.rsqrt(jnp.mean(od * od, axis=-1, keepdims=True) + EPS)
        o_ref[:, hh * V_HEAD_DIM:(hh + 1) * V_HEAD_DIM] = (
            (y * g_ref[hh]) * (1.0 - lam_init)).astype(o_ref.dtype)


def _diff_attention(q, k, vt, lam_vec, head_g, *, batch, seq, lam_init, tq, heads):
    T, attn_width = q.shape
    tk = vt.shape[2]
    n_heads = attn_width // V_HEAD_DIM
    nq = seq // tq
    nk = seq // tk
    width = heads * V_HEAD_DIM
    kern = functools.partial(_attn_kernel, tq=tq, tk=tk, heads=heads, lam_init=lam_init)
    q_spec = pl.BlockSpec((tq, width), lambda b, h, i: (b * nq + i, h))
    return pl.pallas_call(
        kern,
        grid=(batch, n_heads // heads, nq),
        in_specs=[
            pl.BlockSpec(lam_vec.shape, lambda b, h, i: (0, 0)),
            pl.BlockSpec((heads, 1, V_HEAD_DIM), lambda b, h, i: (h, 0, 0)),
            q_spec,
            pl.BlockSpec((seq, width), lambda b, h, i: (b, h)),
            pl.BlockSpec((nk, width, tk), lambda b, h, i: (b, h, 0)),
        ],
        out_specs=q_spec,
        out_shape=jax.ShapeDtypeStruct((T, attn_width), BF16),
        scratch_shapes=[
            pltpu.VMEM((heads, SUBLANES, 2 * tq), F32),
            pltpu.VMEM((heads, ACC_ROWS, 2 * tq), F32),
            pltpu.VMEM((heads, tk, 2 * tq), F32),
            pltpu.VMEM((heads, tk, 2 * tq), F32),
        ],
        compiler_params=pltpu.CompilerParams(
            dimension_semantics=("arbitrary", "arbitrary", "arbitrary"),
            vmem_limit_bytes=52 * 1024 * 1024),
        name="diff_attention",
    )(lam_vec, head_g, q, k, vt)


def _ffn_kernel(x_ref, pool_ref, attn_ref, wo_ref, gf_ref, wup_ref, cw_ref, cb_ref, wd_ref,
                gl_ref, o_ref, carry_ref, x1_ref, h_ref, a_ref, u00_ref, u01_ref, u10_ref, u11_ref,
                *, tm, slab, chunk, tiles_per_seq, n_chunks, apply_final):
    u_refs = ((u00_ref, u01_ref), (u10_ref, u11_ref))
    seq_tile = pl.program_id(0) % tiles_per_seq

    @pl.when(seq_tile == 0)
    def _():
        carry_ref[...] = jnp.zeros_like(carry_ref)

    mixed = jnp.concatenate([pool_ref[...], attn_ref[...]], axis=1)
    x1 = x_ref[...] + _dot(mixed, wo_ref[...])
    x1_ref[...] = x1
    ms = jnp.mean(x1 * x1, axis=-1, keepdims=True)
    h_ref[...] = ((x1 * lax.rsqrt(ms + EPS)) * gf_ref[...]).astype(BF16)

    n_slabs = tm // slab
    cols = lambda idx: slice(idx * chunk, (idx + 1) * chunk)

    def up_matmul(c, r):
        h = h_ref[r * slab:(r + 1) * slab, :]
        return [_dot(h, wup_ref[:, cols(idx)]) for idx in (c, n_chunks + c)]

    def up_store(c, slot, r, us):
        for br, idx in enumerate((c, n_chunks + c)):
            u_ref = u_refs[slot][br]
            if r == 0:
                u_ref[0:CONV_HALO, :] = carry_ref[:, cols(idx)]
            u_ref[CONV_HALO + r * slab:CONV_HALO + (r + 1) * slab, :] = us[br]
            if r == n_slabs - 1:
                carry_ref[:, cols(idx)] = u_ref[tm:, :]

    def conv_branch(slot, br, idx, r):
        r0 = CONV_HALO + r * slab
        u_ref = u_refs[slot][br]
        u = u_ref[r0:r0 + slab, :]
        u1 = u_ref[r0 - 1:r0 - 1 + slab, :]
        u2 = u_ref[r0 - 2:r0 - 2 + slab, :]
        cw = cw_ref[:, cols(idx)]
        return ((cb_ref[:, cols(idx)] + cw[0:1] * u2) + cw[1:2] * u1) + cw[2:3] * u

    for r in range(n_slabs):
        up_store(0, 0, r, up_matmul(0, r))
    for c in range(n_chunks):
        slot = c % 2
        for r in range(n_slabs):
            nxt = up_matmul(c + 1, r) if c + 1 < n_chunks else None
            gate = conv_branch(slot, 0, c, r)
            val = conv_branch(slot, 1, n_chunks + c, r)
            act = (gate * jax.nn.sigmoid(gate)) * val
            a_ref[r * slab:(r + 1) * slab, cols(c)] = act.astype(BF16)
            if nxt is not None:
                up_store(c + 1, 1 - slot, r, nxt)
    ffn_out = _dot(a_ref[...], wd_ref[...])

    x2 = x1_ref[...] + ffn_out
    if apply_final:
        ms2 = jnp.mean(x2 * x2, axis=-1, keepdims=True)
        x2 = (x2 * lax.rsqrt(ms2 + EPS)) * gl_ref[...]
    o_ref[...] = x2


def _ffn(x2d, pool_out, attn_out, wo, g_ffn, wup, cw, cb, wd, g_final,
         *, seq, tm, chunk, apply_final):
    T, d_model = x2d.shape
    n_chunks = wd.shape[0] // chunk
    kern = functools.partial(_ffn_kernel, tm=tm, tiles_per_seq=seq // tm,
                             slab=MXU_DIM, chunk=chunk, n_chunks=n_chunks, apply_final=apply_final)
    row = lambda i: (i, 0)
    const2 = lambda i: (0, 0)
    resident = dict(pipeline_mode=pl.Buffered(1))
    return pl.pallas_call(
        kern,
        grid=(T // tm,),
        in_specs=[
            pl.BlockSpec((tm, d_model), row),
            pl.BlockSpec((tm, pool_out.shape[1]), row),
            pl.BlockSpec((tm, attn_out.shape[1]), row),
            pl.BlockSpec(wo.shape, const2, **resident),
            pl.BlockSpec((1, d_model), const2),
            pl.BlockSpec(wup.shape, const2, **resident),
            pl.BlockSpec(cw.shape, const2),
            pl.BlockSpec(cb.shape, const2),
            pl.BlockSpec(wd.shape, const2, **resident),
            pl.BlockSpec((1, d_model), const2),
        ],
        out_specs=pl.BlockSpec((tm, d_model), row),
        out_shape=jax.ShapeDtypeStruct((T, d_model), F32),
        scratch_shapes=[
            pltpu.VMEM((CONV_HALO, 2 * n_chunks * chunk), F32),
            pltpu.VMEM((tm, d_model), F32),
            pltpu.VMEM((tm, d_model), BF16),
            pltpu.VMEM((tm, n_chunks * chunk), BF16),
        ] + [pltpu.VMEM((CONV_HALO + tm, chunk), F32)] * 4,
        compiler_params=pltpu.CompilerParams(
            dimension_semantics=("arbitrary",),
            vmem_limit_bytes=56 * 1024 * 1024),
        name="outproj_convffn",
    )(x2d, pool_out, attn_out, wo, g_ffn, wup, cw, cb, wd, g_final)


def kernel(x, positions, norm_mix_g, w_in, pool_w, pool_scale, lambda_q1, lambda_k1, lambda_q2,
           lambda_k2, attn_norm_g, w_o, norm_ffn_g, w_up, conv_w, conv_b, w_down, norm_final_g):
    batch, seq, d_model = x.shape
    depth = w_in.shape[0]
    T = batch * seq
    d_ff = w_down.shape[1]
    n_heads = attn_norm_g.shape[1]
    chunk = MXU_DIM
    assert d_ff % chunk == 0 and seq % 512 == 0

    x2d = x.reshape(T, d_model)
    tm_in = 512
    half = DIFF_HEAD_DIM // 2
    groups = LANES // half
    pos_c = positions.astype(F32).reshape(T // tm_in, groups, tm_in // groups).transpose(0, 2, 1)
    pos_c = jnp.broadcast_to(pos_c[..., None], pos_c.shape + (half,)).reshape(T // groups, LANES)
    inv_freq = ROPE_THETA ** (-jnp.arange(0, DIFF_HEAD_DIM, 2, dtype=F32) / DIFF_HEAD_DIM)
    invf = jnp.tile(inv_freq, groups).reshape(1, LANES)

    for l in range(depth):
        lam_init = 0.8 - 0.6 * float(np.exp(-0.3 * l))
        pool_out, q, k, vt = _inproj(
            x2d, pos_c, invf, norm_mix_g[l].reshape(1, d_model), w_in[l].astype(BF16),
            pool_w[l].astype(BF16), pool_scale[l].reshape(1, -1), seq=seq, tm=tm_in)
        lam_vec = jnp.stack([lambda_q1[l], lambda_k1[l], lambda_q2[l], lambda_k2[l]]).astype(F32)
        attn_out = _diff_attention(
            q, k, vt, lam_vec, attn_norm_g[l].reshape(n_heads, 1, V_HEAD_DIM),
            batch=batch, seq=seq, lam_init=lam_init, tq=512, heads=4)
        x2d = _ffn(
            x2d, pool_out, attn_out, w_o[l].astype(BF16), norm_ffn_g[l].reshape(1, d_model),
            w_up[l].astype(BF16), conv_w[l], conv_b[l].reshape(1, -1), w_down[l].astype(BF16),
            norm_final_g.reshape(1, d_model),
            seq=seq, tm=512, chunk=chunk, apply_final=(l == depth - 1))
    return x2d.reshape(batch, seq, d_model)
```

```python
import functools

import numpy as np
import jax
import jax.numpy as jnp
from jax import lax
from jax.experimental import pallas as pl
from jax.experimental.pallas import tpu as pltpu

CHUNK = 64
CHUNK_LOG2 = CHUNK.bit_length() - 1
POOL_WINDOWS = (2, 4, 8, 16)
POOL_GROUP_DIM = 128
DIFF_HEAD_DIM = 64
V_HEAD_DIM = 2 * DIFF_HEAD_DIM
CONV_WIDTH = 3
ROPE_THETA = 10000.0
EPS = 1e-6
LOG2_E = 1.4426950408889634

LANES = 128
SUBLANES = 8
MXU_DIM = 256

POOL_HALO = 16
CONV_HALO = SUBLANES
ONES_ROWS = 16
ACC_ROWS = V_HEAD_DIM + ONES_ROWS
MAX_WAYS = 8

BF16 = jnp.bfloat16
F32 = jnp.float32


def _dot(a, b):
    return jnp.dot(a, b, preferred_element_type=F32)


def _shift_rows(a, k):
    return pltpu.roll(a, k, axis=0)


def _inproj_kernel(x_ref, pos_ref, invf_ref, g_ref, w_ref, pw_ref, ps_ref,
                   pool_ref, q_ref, k_ref, v_ref, carry_ref, *, tm, tiles_per_seq,
                   pool_width, attn_width):
    seq_tile = pl.program_id(0) % tiles_per_seq

    @pl.when(seq_tile == 0)
    def _():
        carry_ref[...] = jnp.zeros_like(carry_ref)

    x = x_ref[...]
    ms = jnp.mean(x * x, axis=-1, keepdims=True)
    h = ((x * lax.rsqrt(ms + EPS)) * g_ref[...]).astype(BF16)

    q0 = pool_width
    k0 = pool_width + attn_width
    v0 = pool_width + 2 * attn_width
    p = _dot(h, w_ref[:, 0:pool_width])
    q = _dot(h, w_ref[:, q0:k0])
    k = _dot(h, w_ref[:, k0:v0])
    v = _dot(h, w_ref[:, v0:v0 + attn_width])

    t = seq_tile * tm + lax.broadcasted_iota(jnp.int32, (tm, POOL_GROUP_DIM), 0)
    pooled_out = []
    for g, w in enumerate(POOL_WINDOWS):
        cols = slice(g * POOL_GROUP_DIM, (g + 1) * POOL_GROUP_DIM)
        pg = p[:, cols]
        s = jnp.concatenate([carry_ref[:, cols], pg], axis=0)
        span = 1
        while span < w:
            s = s + _shift_rows(s, span)
            span *= 2
        count = jnp.minimum(t + 1, w).astype(F32)
        d = (s[POOL_HALO:] / count - pg).astype(BF16)
        pooled_out.append(_dot(d, pw_ref[g]))
    carry_ref[...] = p[tm - POOL_HALO:, :]
    pool_ref[...] = (jnp.concatenate(pooled_out, axis=1) * ps_ref[...]).astype(pool_ref.dtype)

    half = DIFF_HEAD_DIM // 2
    groups = LANES // half
    tq = tm // groups
    ang = pos_ref[...] * invf_ref[...]
    cos_c = jnp.cos(ang)
    sin_c = jnp.sin(ang)
    lane = lax.broadcasted_iota(jnp.int32, (tq, LANES), 1)
    lane_group = lane // half
    first_half = (lane % DIFF_HEAD_DIM) < half
    sign = jnp.where(first_half, -1.0, 1.0).astype(F32)

    def spread(x, a):
        y = None
        for g in range(groups):
            shift = ((g - a) * half) % LANES
            xr = x if shift == 0 else pltpu.roll(x, shift, axis=1)
            y = xr if y is None else jnp.where(lane_group == g, xr, y)
        return y

    tables = [(spread(cos_c, a), spread(sin_c, a) * sign) for a in range(groups)]

    def rope(tt, scale):
        blocks = []
        for a, (cos, sin) in enumerate(tables):
            outs = []
            for j in range(attn_width // LANES):
                tj = tt[a * tq:(a + 1) * tq, j * LANES:(j + 1) * LANES]
                partner = jnp.where(first_half,
                                    pltpu.roll(tj, LANES - half, axis=1),
                                    pltpu.roll(tj, half, axis=1))
                outs.append((tj * cos + partner * sin) * scale)
            blocks.append(jnp.concatenate(outs, axis=1))
        return jnp.concatenate(blocks, axis=0)

    q_ref[...] = rope(q, DIFF_HEAD_DIM ** -0.5 * LOG2_E).astype(q_ref.dtype)
    k_ref[...] = rope(k, 1.0).astype(k_ref.dtype)
    v_ref[0] = v.T.astype(v_ref.dtype)


def _inproj(x2, pos_b, invf, g, w, pool_w, pool_scale, *, seq, tm):
    T, d_model = x2.shape
    pool_width = pool_scale.shape[-1]
    attn_width = (w.shape[1] - pool_width) // 3
    n_groups = pool_w.shape[0]
    kern = functools.partial(_inproj_kernel, tm=tm, tiles_per_seq=seq // tm,
                             pool_width=pool_width, attn_width=attn_width)
    row = lambda i: (i, 0)
    const2 = lambda i: (0, 0)
    out_sds = lambda width: jax.ShapeDtypeStruct((T, width), BF16)
    return pl.pallas_call(
        kern,
        grid=(T // tm,),
        in_specs=[
            pl.BlockSpec((tm, d_model), row),
            pl.BlockSpec((tm * (DIFF_HEAD_DIM // 2) // LANES, LANES), row),
            pl.BlockSpec((1, LANES), const2),
            pl.BlockSpec((1, d_model), const2),
            pl.BlockSpec(w.shape, const2),
            pl.BlockSpec(pool_w.shape, lambda i: (0, 0, 0)),
            pl.BlockSpec((1, pool_width), const2),
        ],
        out_specs=[
            pl.BlockSpec((tm, pool_width), row),
            pl.BlockSpec((tm, attn_width), row),
            pl.BlockSpec((tm, attn_width), row),
            pl.BlockSpec((1, attn_width, tm), lambda i: (i, 0, 0)),
        ],
        out_shape=[out_sds(pool_width), out_sds(attn_width), out_sds(attn_width),
                   jax.ShapeDtypeStruct((T // tm, attn_width, tm), BF16)],
        scratch_shapes=[pltpu.VMEM((POOL_HALO, n_groups * POOL_GROUP_DIM), F32)],
        compiler_params=pltpu.CompilerParams(
            dimension_semantics=("arbitrary",),
            vmem_limit_bytes=48 * 1024 * 1024),
        name="inproj_pool_rope",
    )(x2, pos_b, invf, g, w, pool_w, pool_scale)


def _attn_kernel(lam_ref, g_ref, q_ref, k_ref, vt_ref, o_ref, m_ref, acc_ref, sa_ref, sb_ref,
                 *, tq, tk, heads, lam_init):
    i = pl.program_id(2)
    cols = 2 * tq
    groups = tk // SUBLANES

    qqts = []

    def scores(hh, j):
        ch = slice(hh * V_HEAD_DIM, (hh + 1) * V_HEAD_DIM)
        kt = k_ref[pl.ds(pl.multiple_of(j * tk, tk), tk), ch]
        return _dot(kt, qqts[hh])

    def prologue():
        m_ref[...] = jnp.full(m_ref.shape, -jnp.inf, F32)
        acc_ref[...] = jnp.zeros(acc_ref.shape, F32)
        qqts.clear()
        sub = lax.broadcasted_iota(jnp.int32, (V_HEAD_DIM, tq), 0)
        for hh in range(heads):
            qt = q_ref[:, hh * V_HEAD_DIM:(hh + 1) * V_HEAD_DIM].astype(F32).T
            zero = jnp.zeros_like(qt)
            qqts.append(jnp.concatenate([jnp.where(sub < DIFF_HEAD_DIM, qt, zero),
                                         jnp.where(sub >= DIFF_HEAD_DIM, qt, zero)],
                                        axis=1).astype(BF16))
            sa_ref[hh] = scores(hh, 0)

    def softmax_pv(hh, j, s, masked):
        ch = slice(hh * V_HEAD_DIM, (hh + 1) * V_HEAD_DIM)
        if masked:
            c = lax.broadcasted_iota(jnp.int32, (SUBLANES, cols), 1)
            q_chunk = (i * tq - j * tk + jnp.where(c >= tq, c - tq, c)) >> CHUNK_LOG2
            blocks = []
            for kc in range(tk // CHUNK):
                blk = s[kc * CHUNK:(kc + 1) * CHUNK].reshape(CHUNK // SUBLANES, SUBLANES, cols)
                blocks.append(jnp.where((q_chunk >= kc)[None], blk, -jnp.inf).reshape(CHUNK, cols))
            s = jnp.concatenate(blocks, axis=0)
        s4 = s.reshape(MAX_WAYS, groups // MAX_WAYS, SUBLANES, cols)
        m_tile = jnp.max(jnp.max(jnp.max(s4, axis=1), axis=0), axis=0, keepdims=True)
        m_prev = m_ref[hh]
        m_new = jnp.maximum(m_prev, m_tile)
        alpha = jnp.exp2(m_prev - m_new)
        p = jnp.exp2(s.reshape(groups, SUBLANES, cols) - m_new[None]).reshape(tk, cols).astype(BF16)
        lhs = jnp.concatenate([vt_ref[j, ch, :], jnp.ones((ONES_ROWS, tk), BF16)], axis=0)
        pv = _dot(lhs, p)
        acc3 = acc_ref[hh].reshape(ACC_ROWS // SUBLANES, SUBLANES, cols)
        acc_ref[hh] = (alpha[None] * acc3).reshape(ACC_ROWS, cols) + pv
        m_ref[hh] = m_new

    def store_scores(j, buf):
        for hh in range(heads):
            buf[hh] = scores(hh, j)

    def full_step(j, cur, nxt):
        store_scores(j + 1, nxt)
        for hh in range(heads):
            softmax_pv(hh, j, cur[hh], masked=False)

    def last_step(j, cur):
        for hh in range(heads):
            softmax_pv(hh, j, cur[hh], masked=True)

    def finalize():
        lam_vec = lam_ref[...]
        lam = (jnp.exp(jnp.sum(lam_vec[0:1] * lam_vec[1:2], axis=1, keepdims=True))
               - jnp.exp(jnp.sum(lam_vec[2:3] * lam_vec[3:4], axis=1, keepdims=True))
               + lam_init)
        for hh in range(heads):
            acc = acc_ref[hh]
            ot = acc[:V_HEAD_DIM] / acc[V_HEAD_DIM:V_HEAD_DIM + 1]
            od = (ot[:, :tq] - lam * ot[:, tq:]).T
            y = od * lax.rsqrt(jnp.mean(od * od, axis=-1, keepdims=True) + EPS)
            o_ref[:, hh * V_HEAD_DIM:(hh + 1) * V_HEAD_DIM] = (
                (y * g_ref[hh]) * (1.0 - lam_init)).astype(o_ref.dtype)

    n_full = (i * tq) // tk

    @pl.when(n_full == 0)
    def _():
        prologue()
        last_step(0, sa_ref)
        finalize()

    @pl.when(n_full > 0)
    def _():
        prologue()
        full_step(0, sa_ref, sb_ref)
        rest = n_full - 1

        def pair(t, carry):
            full_step(2 * t + 1, sb_ref, sa_ref)
            full_step(2 * t + 2, sa_ref, sb_ref)
            return carry

        lax.fori_loop(0, rest // 2, pair, 0)

        @pl.when(rest % 2 == 1)
        def _():
            full_step(n_full - 1, sb_ref, sa_ref)
            last_step(n_full, sa_ref)

        @pl.when(rest % 2 == 0)
        def _():
            last_step(n_full, sb_ref)

        finalize()


def _diff_attention(q, k, vt, lam_vec, head_g, *, batch, seq, lam_init, tq, heads):
    T, attn_width = q.shape
    tk = vt.shape[2]
    n_heads = attn_width // V_HEAD_DIM
    nq = seq // tq
    nk = seq // tk
    width = heads * V_HEAD_DIM
    kern = functools.partial(_attn_kernel, tq=tq, tk=tk, heads=heads, lam_init=lam_init)
    q_spec = pl.BlockSpec((tq, width), lambda b, h, i: (b * nq + i, h))
    return pl.pallas_call(
        kern,
        grid=(batch, n_heads // heads, nq),
        in_specs=[
            pl.BlockSpec(lam_vec.shape, lambda b, h, i: (0, 0)),
            pl.BlockSpec((heads, 1, V_HEAD_DIM), lambda b, h, i: (h, 0, 0)),
            q_spec,
            pl.BlockSpec((seq, width), lambda b, h, i: (b, h)),
            pl.BlockSpec((nk, width, tk), lambda b, h, i: (b, h, 0)),
        ],
        out_specs=q_spec,
        out_shape=jax.ShapeDtypeStruct((T, attn_width), BF16),
        scratch_shapes=[
            pltpu.VMEM((heads, SUBLANES, 2 * tq), F32),
            pltpu.VMEM((heads, ACC_ROWS, 2 * tq), F32),
            pltpu.VMEM((heads, tk, 2 * tq), F32),
            pltpu.VMEM((heads, tk, 2 * tq), F32),
        ],
        compiler_params=pltpu.CompilerParams(
            dimension_semantics=("arbitrary", "arbitrary", "arbitrary"),
            vmem_limit_bytes=52 * 1024 * 1024),
        name="diff_attention",
    )(lam_vec, head_g, q, k, vt)


def _ffn_kernel(x_ref, pool_ref, attn_ref, wo_ref, gf_ref, wup_ref, cw_ref, cb_ref, wd_ref,
                gl_ref, o_ref, carry_ref, x1_ref, h_ref, a_ref, u00_ref, u01_ref, u10_ref, u11_ref,
                *, tm, slab, chunk, tiles_per_seq, n_chunks, apply_final):
    u_refs = ((u00_ref, u01_ref), (u10_ref, u11_ref))
    seq_tile = pl.program_id(0) % tiles_per_seq

    @pl.when(seq_tile == 0)
    def _():
        carry_ref[...] = jnp.zeros_like(carry_ref)

    mixed = jnp.concatenate([pool_ref[...], attn_ref[...]], axis=1)
    x1 = x_ref[...] + _dot(mixed, wo_ref[...])
    x1_ref[...] = x1
    ms = jnp.mean(x1 * x1, axis=-1, keepdims=True)
    h_ref[...] = ((x1 * lax.rsqrt(ms + EPS)) * gf_ref[...]).astype(BF16)

    n_slabs = tm // slab
    cols = lambda idx: slice(idx * chunk, (idx + 1) * chunk)

    def up_matmul(c, r):
        h = h_ref[r * slab:(r + 1) * slab, :]
        return [_dot(h, wup_ref[:, cols(idx)]) for idx in (c, n_chunks + c)]

    def up_store(c, slot, r, us):
        for br, idx in enumerate((c, n_chunks + c)):
            u_ref = u_refs[slot][br]
            if r == 0:
                u_ref[0:CONV_HALO, :] = carry_ref[:, cols(idx)]
            u_ref[CONV_HALO + r * slab:CONV_HALO + (r + 1) * slab, :] = us[br]
            if r == n_slabs - 1:
                carry_ref[:, cols(idx)] = u_ref[tm:, :]

    def conv_branch(slot, br, idx, r):
        r0 = CONV_HALO + r * slab
        u_ref = u_refs[slot][br]
        u = u_ref[r0:r0 + slab, :]
        u1 = u_ref[r0 - 1:r0 - 1 + slab, :]
        u2 = u_ref[r0 - 2:r0 - 2 + slab, :]
        cw = cw_ref[:, cols(idx)]
        return ((cb_ref[:, cols(idx)] + cw[0:1] * u2) + cw[1:2] * u1) + cw[2:3] * u

    for r in range(n_slabs):
        up_store(0, 0, r, up_matmul(0, r))
    for c in range(n_chunks):
        slot = c % 2
        for r in range(n_slabs):
            nxt = up_matmul(c + 1, r) if c + 1 < n_chunks else None
            gate = conv_branch(slot, 0, c, r)
            val = conv_branch(slot, 1, n_chunks + c, r)
            act = (gate * jax.nn.sigmoid(gate)) * val
            a_ref[r * slab:(r + 1) * slab, cols(c)] = act.astype(BF16)
            if nxt is not None:
                up_store(c + 1, 1 - slot, r, nxt)
    ffn_out = _dot(a_ref[...], wd_ref[...])

    x2 = x1_ref[...] + ffn_out
    if apply_final:
        ms2 = jnp.mean(x2 * x2, axis=-1, keepdims=True)
        x2 = (x2 * lax.rsqrt(ms2 + EPS)) * gl_ref[...]
    o_ref[...] = x2


def _ffn(x2d, pool_out, attn_out, wo, g_ffn, wup, cw, cb, wd, g_final,
         *, seq, tm, chunk, apply_final):
    T, d_model = x2d.shape
    n_chunks = wd.shape[0] // chunk
    kern = functools.partial(_ffn_kernel, tm=tm, tiles_per_seq=seq // tm,
                             slab=MXU_DIM, chunk=chunk, n_chunks=n_chunks, apply_final=apply_final)
    row = lambda i: (i, 0)
    const2 = lambda i: (0, 0)
    resident = dict(pipeline_mode=pl.Buffered(1))
    return pl.pallas_call(
        kern,
        grid=(T // tm,),
        in_specs=[
            pl.BlockSpec((tm, d_model), row),
            pl.BlockSpec((tm, pool_out.shape[1]), row),
            pl.BlockSpec((tm, attn_out.shape[1]), row),
            pl.BlockSpec(wo.shape, const2, **resident),
            pl.BlockSpec((1, d_model), const2),
            pl.BlockSpec(wup.shape, const2, **resident),
            pl.BlockSpec(cw.shape, const2),
            pl.BlockSpec(cb.shape, const2),
            pl.BlockSpec(wd.shape, const2, **resident),
            pl.BlockSpec((1, d_model), const2),
        ],
        out_specs=pl.BlockSpec((tm, d_model), row),
        out_shape=jax.ShapeDtypeStruct((T, d_model), F32),
        scratch_shapes=[
            pltpu.VMEM((CONV_HALO, 2 * n_chunks * chunk), F32),
            pltpu.VMEM((tm, d_model), F32),
            pltpu.VMEM((tm, d_model), BF16),
            pltpu.VMEM((tm, n_chunks * chunk), BF16),
        ] + [pltpu.VMEM((CONV_HALO + tm, chunk), F32)] * 4,
        compiler_params=pltpu.CompilerParams(
            dimension_semantics=("arbitrary",),
            vmem_limit_bytes=56 * 1024 * 1024),
        name="outproj_convffn",
    )(x2d, pool_out, attn_out, wo, g_ffn, wup, cw, cb, wd, g_final)


def kernel(x, positions, norm_mix_g, w_in, pool_w, pool_scale, lambda_q1, lambda_k1, lambda_q2,
           lambda_k2, attn_norm_g, w_o, norm_ffn_g, w_up, conv_w, conv_b, w_down, norm_final_g):
    batch, seq, d_model = x.shape
    depth = w_in.shape[0]
    T = batch * seq
    d_ff = w_down.shape[1]
    n_heads = attn_norm_g.shape[1]
    chunk = MXU_DIM
    assert d_ff % chunk == 0 and seq % 512 == 0

    x2d = x.reshape(T, d_model)
    tm_in = 512
    half = DIFF_HEAD_DIM // 2
    groups = LANES // half
    pos_c = positions.astype(F32).reshape(T // tm_in, groups, tm_in // groups).transpose(0, 2, 1)
    pos_c = jnp.broadcast_to(pos_c[..., None], pos_c.shape + (half,)).reshape(T // groups, LANES)
    inv_freq = ROPE_THETA ** (-jnp.arange(0, DIFF_HEAD_DIM, 2, dtype=F32) / DIFF_HEAD_DIM)
    invf = jnp.tile(inv_freq, groups).reshape(1, LANES)

    for l in range(depth):
        lam_init = 0.8 - 0.6 * float(np.exp(-0.3 * l))
        pool_out, q, k, vt = _inproj(
            x2d, pos_c, invf, norm_mix_g[l].reshape(1, d_model), w_in[l].astype(BF16),
            pool_w[l].astype(BF16), pool_scale[l].reshape(1, -1), seq=seq, tm=tm_in)
        lam_vec = jnp.stack([lambda_q1[l], lambda_k1[l], lambda_q2[l], lambda_k2[l]]).astype(F32)
        attn_out = _diff_attention(
            q, k, vt, lam_vec, attn_norm_g[l].reshape(n_heads, 1, V_HEAD_DIM),
            batch=batch, seq=seq, lam_init=lam_init, tq=512, heads=4)
        x2d = _ffn(
            x2d, pool_out, attn_out, w_o[l].astype(BF16), norm_ffn_g[l].reshape(1, d_model),
            w_up[l].astype(BF16), conv_w[l], conv_b[l].reshape(1, -1), w_down[l].astype(BF16),
            norm_final_g.reshape(1, d_model),
            seq=seq, tm=512, chunk=chunk, apply_final=(l == depth - 1))
    return x2d.reshape(batch, seq, d_model)
```

```python
import functools

import numpy as np
import jax
import jax.numpy as jnp
from jax import lax
from jax.experimental import pallas as pl
from jax.experimental.pallas import tpu as pltpu

CHUNK = 64
CHUNK_LOG2 = CHUNK.bit_length() - 1
POOL_WINDOWS = (2, 4, 8, 16)
POOL_GROUP_DIM = 128
DIFF_HEAD_DIM = 64
V_HEAD_DIM = 2 * DIFF_HEAD_DIM
CONV_WIDTH = 3
ROPE_THETA = 10000.0
EPS = 1e-6
LOG2_E = 1.4426950408889634

LANES = 128
SUBLANES = 8
MXU_DIM = 256

POOL_HALO = 16
CONV_HALO = SUBLANES
ONES_ROWS = 16
ACC_ROWS = V_HEAD_DIM + ONES_ROWS
MAX_WAYS = 8

BF16 = jnp.bfloat16
F32 = jnp.float32


def _dot(a, b):
    return jnp.dot(a, b, preferred_element_type=F32)


def _shift_rows(a, k):
    return pltpu.roll(a, k, axis=0)


def _inproj_kernel(x_ref, pos_ref, invf_ref, g_ref, w_ref, pw_ref, ps_ref,
                   pool_ref, q_ref, k_ref, v_ref, carry_ref, *, tm, tiles_per_seq,
                   pool_width, attn_width):
    seq_tile = pl.program_id(0) % tiles_per_seq

    @pl.when(seq_tile == 0)
    def _():
        carry_ref[...] = jnp.zeros_like(carry_ref)

    x = x_ref[...]
    ms = jnp.mean(x * x, axis=-1, keepdims=True)
    h = ((x * lax.rsqrt(ms + EPS)) * g_ref[...]).astype(BF16)

    q0 = pool_width
    k0 = pool_width + attn_width
    v0 = pool_width + 2 * attn_width
    p = _dot(h, w_ref[:, 0:pool_width])
    q = _dot(h, w_ref[:, q0:k0])
    k = _dot(h, w_ref[:, k0:v0])
    v = _dot(h, w_ref[:, v0:v0 + attn_width])

    t = seq_tile * tm + lax.broadcasted_iota(jnp.int32, (tm, POOL_GROUP_DIM), 0)
    pooled_out = []
    for g, w in enumerate(POOL_WINDOWS):
        cols = slice(g * POOL_GROUP_DIM, (g + 1) * POOL_GROUP_DIM)
        pg = p[:, cols]
        s = jnp.concatenate([carry_ref[:, cols], pg], axis=0)
        span = 1
        while span < w:
            s = s + _shift_rows(s, span)
            span *= 2
        count = jnp.minimum(t + 1, w).astype(F32)
        d = (s[POOL_HALO:] / count - pg).astype(BF16)
        pooled_out.append(_dot(d, pw_ref[g]))
    carry_ref[...] = p[tm - POOL_HALO:, :]
    pool_ref[...] = (jnp.concatenate(pooled_out, axis=1) * ps_ref[...]).astype(pool_ref.dtype)

    half = DIFF_HEAD_DIM // 2
    groups = LANES // half
    tq = tm // groups
    ang = pos_ref[...] * invf_ref[...]
    cos_c = jnp.cos(ang)
    sin_c = jnp.sin(ang)
    lane = lax.broadcasted_iota(jnp.int32, (tq, LANES), 1)
    lane_group = lane // half
    first_half = (lane % DIFF_HEAD_DIM) < half
    sign = jnp.where(first_half, -1.0, 1.0).astype(F32)

    def spread(x, a):
        y = None
        for g in range(groups):
            shift = ((g - a) * half) % LANES
            xr = x if shift == 0 else pltpu.roll(x, shift, axis=1)
            y = xr if y is None else jnp.where(lane_group == g, xr, y)
        return y

    tables = [(spread(cos_c, a), spread(sin_c, a) * sign) for a in range(groups)]

    def rope(tt, scale):
        blocks = []
        for a, (cos, sin) in enumerate(tables):
            outs = []
            for j in range(attn_width // LANES):
                tj = tt[a * tq:(a + 1) * tq, j * LANES:(j + 1) * LANES]
                partner = jnp.where(first_half,
                                    pltpu.roll(tj, LANES - half, axis=1),
                                    pltpu.roll(tj, half, axis=1))
                outs.append((tj * cos + partner * sin) * scale)
            blocks.append(jnp.concatenate(outs, axis=1))
        return jnp.concatenate(blocks, axis=0)

    q_ref[...] = rope(q, DIFF_HEAD_DIM ** -0.5 * LOG2_E).astype(q_ref.dtype)
    k_ref[...] = rope(k, 1.0).astype(k_ref.dtype)
    v_ref[0] = v.T.astype(v_ref.dtype)


def _inproj(x2, pos_b, invf, g, w, pool_w, pool_scale, *, seq, tm):
    T, d_model = x2.shape
    pool_width = pool_scale.shape[-1]
    attn_width = (w.shape[1] - pool_width) // 3
    n_groups = pool_w.shape[0]
    kern = functools.partial(_inproj_kernel, tm=tm, tiles_per_seq=seq // tm,
                             pool_width=pool_width, attn_width=attn_width)
    row = lambda i: (i, 0)
    const2 = lambda i: (0, 0)
    out_sds = lambda width: jax.ShapeDtypeStruct((T, width), BF16)
    return pl.pallas_call(
        kern,
        grid=(T // tm,),
        in_specs=[
            pl.BlockSpec((tm, d_model), row),
            pl.BlockSpec((tm * (DIFF_HEAD_DIM // 2) // LANES, LANES), row),
            pl.BlockSpec((1, LANES), const2),
            pl.BlockSpec((1, d_model), const2),
            pl.BlockSpec(w.shape, const2),
            pl.BlockSpec(pool_w.shape, lambda i: (0, 0, 0)),
            pl.BlockSpec((1, pool_width), const2),
        ],
        out_specs=[
            pl.BlockSpec((tm, pool_width), row),
            pl.BlockSpec((tm, attn_width), row),
            pl.BlockSpec((tm, attn_width), row),
            pl.BlockSpec((1, attn_width, tm), lambda i: (i, 0, 0)),
        ],
        out_shape=[out_sds(pool_width), out_sds(attn_width), out_sds(attn_width),
                   jax.ShapeDtypeStruct((T // tm, attn_width, tm), BF16)],
        scratch_shapes=[pltpu.VMEM((POOL_HALO, n_groups * POOL_GROUP_DIM), F32)],
        compiler_params=pltpu.CompilerParams(
            dimension_semantics=("arbitrary",),
            vmem_limit_bytes=48 * 1024 * 1024),
        name="inproj_pool_rope",
    )(x2, pos_b, invf, g, w, pool_w, pool_scale)


def _attn_kernel(lam_ref, g_ref, q_ref, k_ref, vt_ref, o_ref, m_ref, acc_ref, sa_ref, sb_ref,
                 *, tq, tk, heads, lam_init):
    i = pl.program_id(2)
    cols = 2 * tq
    groups = tk // SUBLANES

    qqts = []

    def scores(hh, j):
        ch = slice(hh * V_HEAD_DIM, (hh + 1) * V_HEAD_DIM)
        kt = k_ref[pl.ds(pl.multiple_of(j * tk, tk), tk), ch]
        return _dot(kt, qqts[hh])

    def prologue():
        m_ref[...] = jnp.full(m_ref.shape, -jnp.inf, F32)
        acc_ref[...] = jnp.zeros(acc_ref.shape, F32)
        qqts.clear()
        sub = lax.broadcasted_iota(jnp.int32, (V_HEAD_DIM, tq), 0)
        for hh in range(heads):
            qt = q_ref[:, hh * V_HEAD_DIM:(hh + 1) * V_HEAD_DIM].astype(F32).T
            zero = jnp.zeros_like(qt)
            qqts.append(jnp.concatenate([jnp.where(sub < DIFF_HEAD_DIM, qt, zero),
                                         jnp.where(sub >= DIFF_HEAD_DIM, qt, zero)],
                                        axis=1).astype(BF16))
            sa_ref[hh] = scores(hh, 0)

    def softmax_pv(hh, j, s, masked):
        ch = slice(hh * V_HEAD_DIM, (hh + 1) * V_HEAD_DIM)
        if masked:
            c = lax.broadcasted_iota(jnp.int32, (SUBLANES, cols), 1)
            q_chunk = (i * tq - j * tk + jnp.where(c >= tq, c - tq, c)) >> CHUNK_LOG2
            blocks = []
            for kc in range(tk // CHUNK):
                blk = s[kc * CHUNK:(kc + 1) * CHUNK].reshape(CHUNK // SUBLANES, SUBLANES, cols)
                blocks.append(jnp.where((q_chunk >= kc)[None], blk, -jnp.inf).reshape(CHUNK, cols))
            s = jnp.concatenate(blocks, axis=0)
        s4 = s.reshape(MAX_WAYS, groups // MAX_WAYS, SUBLANES, cols)
        m_tile = jnp.max(jnp.max(jnp.max(s4, axis=1), axis=0), axis=0, keepdims=True)
        m_prev = m_ref[hh]
        m_new = jnp.maximum(m_prev, m_tile)
        alpha = jnp.exp2(m_prev - m_new)
        p = jnp.exp2(s.reshape(groups, SUBLANES, cols) - m_new[None]).reshape(tk, cols).astype(BF16)
        lhs = jnp.concatenate([vt_ref[j, ch, :], jnp.ones((ONES_ROWS, tk), BF16)], axis=0)
        pv = _dot(lhs, p)
        acc3 = acc_ref[hh].reshape(ACC_ROWS // SUBLANES, SUBLANES, cols)
        acc_ref[hh] = (alpha[None] * acc3).reshape(ACC_ROWS, cols) + pv
        m_ref[hh] = m_new

    def store_scores(j, buf):
        for hh in range(heads):
            buf[hh] = scores(hh, j)

    def full_step(j, cur, nxt):
        store_scores(j + 1, nxt)
        for hh in range(heads):
            softmax_pv(hh, j, cur[hh], masked=False)

    def last_step(j, cur):
        for hh in range(heads):
            softmax_pv(hh, j, cur[hh], masked=True)

    def finalize():
        lam_vec = lam_ref[...]
        lam = (jnp.exp(jnp.sum(lam_vec[0:1] * lam_vec[1:2], axis=1, keepdims=True))
               - jnp.exp(jnp.sum(lam_vec[2:3] * lam_vec[3:4], axis=1, keepdims=True))
               + lam_init)
        for hh in range(heads):
            acc = acc_ref[hh]
            ot = acc[:V_HEAD_DIM] / acc[V_HEAD_DIM:V_HEAD_DIM + 1]
            od = (ot[:, :tq] - lam * ot[:, tq:]).T
            y = od * lax.rsqrt(jnp.mean(od * od, axis=-1, keepdims=True) + EPS)
            o_ref[:, hh * V_HEAD_DIM:(hh + 1) * V_HEAD_DIM] = (
                (y * g_ref[hh]) * (1.0 - lam_init)).astype(o_ref.dtype)

    n_full = (i * tq) // tk

    @pl.when(n_full == 0)
    def _():
        prologue()
        last_step(0, sa_ref)
        finalize()

    @pl.when(n_full > 0)
    def _():
        prologue()
        full_step(0, sa_ref, sb_ref)
        rest = n_full - 1

        def pair(t, carry):
            full_step(2 * t + 1, sb_ref, sa_ref)
            full_step(2 * t + 2, sa_ref, sb_ref)
            return carry

        lax.fori_loop(0, rest // 2, pair, 0)

        @pl.when(rest % 2 == 1)
        def _():
            full_step(n_full - 1, sb_ref, sa_ref)
            last_step(n_full, sa_ref)
            finalize()

        @pl.when(rest % 2 == 0)
        def _():
            last_step(n_full, sb_ref)
            finalize()


def _diff_attention(q, k, vt, lam_vec, head_g, *, batch, seq, lam_init, tq, heads):
    T, attn_width = q.shape
    tk = vt.shape[2]
    n_heads = attn_width // V_HEAD_DIM
    nq = seq // tq
    nk = seq // tk
    width = heads * V_HEAD_DIM
    kern = functools.partial(_attn_kernel, tq=tq, tk=tk, heads=heads, lam_init=lam_init)
    q_spec = pl.BlockSpec((tq, width), lambda b, h, i: (b * nq + i, h))
    return pl.pallas_call(
        kern,
        grid=(batch, n_heads // heads, nq),
        in_specs=[
            pl.BlockSpec(lam_vec.shape, lambda b, h, i: (0, 0)),
            pl.BlockSpec((heads, 1, V_HEAD_DIM), lambda b, h, i: (h, 0, 0)),
            q_spec,
            pl.BlockSpec((seq, width), lambda b, h, i: (b, h)),
            pl.BlockSpec((nk, width, tk), lambda b, h, i: (b, h, 0)),
        ],
        out_specs=q_spec,
        out_shape=jax.ShapeDtypeStruct((T, attn_width), BF16),
        scratch_shapes=[
            pltpu.VMEM((heads, SUBLANES, 2 * tq), F32),
            pltpu.VMEM((heads, ACC_ROWS, 2 * tq), F32),
            pltpu.VMEM((heads, tk, 2 * tq), F32),
            pltpu.VMEM((heads, tk, 2 * tq), F32),
        ],
        compiler_params=pltpu.CompilerParams(
            dimension_semantics=("arbitrary", "arbitrary", "arbitrary"),
            vmem_limit_bytes=52 * 1024 * 1024),
        name="diff_attention",
    )(lam_vec, head_g, q, k, vt)


def _ffn_kernel(x_ref, pool_ref, attn_ref, wo_ref, gf_ref, wup_ref, cw_ref, cb_ref, wd_ref,
                gl_ref, o_ref, carry_ref, x1_ref, h_ref, a_ref, u00_ref, u01_ref, u10_ref, u11_ref,
                *, tm, slab, chunk, tiles_per_seq, n_chunks, apply_final):
    u_refs = ((u00_ref, u01_ref), (u10_ref, u11_ref))
    seq_tile = pl.program_id(0) % tiles_per_seq

    @pl.when(seq_tile == 0)
    def _():
        carry_ref[...] = jnp.zeros_like(carry_ref)

    mixed = jnp.concatenate([pool_ref[...], attn_ref[...]], axis=1)
    x1 = x_ref[...] + _dot(mixed, wo_ref[...])
    x1_ref[...] = x1
    ms = jnp.mean(x1 * x1, axis=-1, keepdims=True)
    h_ref[...] = ((x1 * lax.rsqrt(ms + EPS)) * gf_ref[...]).astype(BF16)

    n_slabs = tm // slab
    cols = lambda idx: slice(idx * chunk, (idx + 1) * chunk)

    def up_matmul(c, r):
        h = h_ref[r * slab:(r + 1) * slab, :]
        return [_dot(h, wup_ref[:, cols(idx)]) for idx in (c, n_chunks + c)]

    def up_store(c, slot, r, us):
        for br, idx in enumerate((c, n_chunks + c)):
            u_ref = u_refs[slot][br]
            if r == 0:
                u_ref[0:CONV_HALO, :] = carry_ref[:, cols(idx)]
            u_ref[CONV_HALO + r * slab:CONV_HALO + (r + 1) * slab, :] = us[br]
            if r == n_slabs - 1:
                carry_ref[:, cols(idx)] = u_ref[tm:, :]

    def conv_branch(slot, br, idx, r):
        r0 = CONV_HALO + r * slab
        u_ref = u_refs[slot][br]
        ext = u_ref[r0 - SUBLANES:r0 + slab, :].reshape(slab // SUBLANES + 1, SUBLANES, chunk)
        sub = lax.broadcasted_iota(jnp.int32, (1, SUBLANES, chunk), 1)

        def delayed(d):
            rot = pltpu.roll(ext, d, axis=1)
            return jnp.where(sub < d, rot[:-1], rot[1:]).reshape(slab, chunk)

        u = ext[1:].reshape(slab, chunk)
        u1 = delayed(1)
        u2 = delayed(2)
        cw = cw_ref[:, cols(idx)]
        return ((cb_ref[:, cols(idx)] + cw[0:1] * u2) + cw[1:2] * u1) + cw[2:3] * u

    for r in range(n_slabs):
        up_store(0, 0, r, up_matmul(0, r))
    for c in range(n_chunks):
        slot = c % 2
        for r in range(n_slabs):
            nxt = up_matmul(c + 1, r) if c + 1 < n_chunks else None
            gate = conv_branch(slot, 0, c, r)
            val = conv_branch(slot, 1, n_chunks + c, r)
            act = (gate * jax.nn.sigmoid(gate)) * val
            a_ref[r * slab:(r + 1) * slab, cols(c)] = act.astype(BF16)
            if nxt is not None:
                up_store(c + 1, 1 - slot, r, nxt)
    ffn_out = _dot(a_ref[...], wd_ref[...])

    x2 = x1_ref[...] + ffn_out
    if apply_final:
        ms2 = jnp.mean(x2 * x2, axis=-1, keepdims=True)
        x2 = (x2 * lax.rsqrt(ms2 + EPS)) * gl_ref[...]
    o_ref[...] = x2


def _ffn(x2d, pool_out, attn_out, wo, g_ffn, wup, cw, cb, wd, g_final,
         *, seq, tm, chunk, apply_final):
    T, d_model = x2d.shape
    n_chunks = wd.shape[0] // chunk
    kern = functools.partial(_ffn_kernel, tm=tm, tiles_per_seq=seq // tm,
                             slab=MXU_DIM, chunk=chunk, n_chunks=n_chunks, apply_final=apply_final)
    row = lambda i: (i, 0)
    const2 = lambda i: (0, 0)
    resident = dict(pipeline_mode=pl.Buffered(1))
    return pl.pallas_call(
        kern,
        grid=(T // tm,),
        in_specs=[
            pl.BlockSpec((tm, d_model), row),
            pl.BlockSpec((tm, pool_out.shape[1]), row),
            pl.BlockSpec((tm, attn_out.shape[1]), row),
            pl.BlockSpec(wo.shape, const2, **resident),
            pl.BlockSpec((1, d_model), const2),
            pl.BlockSpec(wup.shape, const2, **resident),
            pl.BlockSpec(cw.shape, const2),
            pl.BlockSpec(cb.shape, const2),
            pl.BlockSpec(wd.shape, const2, **resident),
            pl.BlockSpec((1, d_model), const2),
        ],
        out_specs=pl.BlockSpec((tm, d_model), row),
        out_shape=jax.ShapeDtypeStruct((T, d_model), F32),
        scratch_shapes=[
            pltpu.VMEM((CONV_HALO, 2 * n_chunks * chunk), F32),
            pltpu.VMEM((tm, d_model), F32),
            pltpu.VMEM((tm, d_model), BF16),
            pltpu.VMEM((tm, n_chunks * chunk), BF16),
        ] + [pltpu.VMEM((CONV_HALO + tm, chunk), F32)] * 4,
        compiler_params=pltpu.CompilerParams(
            dimension_semantics=("arbitrary",),
            vmem_limit_bytes=56 * 1024 * 1024),
        name="outproj_convffn",
    )(x2d, pool_out, attn_out, wo, g_ffn, wup, cw, cb, wd, g_final)


def kernel(x, positions, norm_mix_g, w_in, pool_w, pool_scale, lambda_q1, lambda_k1, lambda_q2,
           lambda_k2, attn_norm_g, w_o, norm_ffn_g, w_up, conv_w, conv_b, w_down, norm_final_g):
    batch, seq, d_model = x.shape
    depth = w_in.shape[0]
    T = batch * seq
    d_ff = w_down.shape[1]
    n_heads = attn_norm_g.shape[1]
    chunk = MXU_DIM
    assert d_ff % chunk == 0 and seq % 512 == 0

    x2d = x.reshape(T, d_model)
    tm_in = 512
    half = DIFF_HEAD_DIM // 2
    groups = LANES // half
    pos_c = positions.astype(F32).reshape(T // tm_in, groups, tm_in // groups).transpose(0, 2, 1)
    pos_c = jnp.broadcast_to(pos_c[..., None], pos_c.shape + (half,)).reshape(T // groups, LANES)
    inv_freq = ROPE_THETA ** (-jnp.arange(0, DIFF_HEAD_DIM, 2, dtype=F32) / DIFF_HEAD_DIM)
    invf = jnp.tile(inv_freq, groups).reshape(1, LANES)

    for l in range(depth):
        lam_init = 0.8 - 0.6 * float(np.exp(-0.3 * l))
        pool_out, q, k, vt = _inproj(
            x2d, pos_c, invf, norm_mix_g[l].reshape(1, d_model), w_in[l].astype(BF16),
            pool_w[l].astype(BF16), pool_scale[l].reshape(1, -1), seq=seq, tm=tm_in)
        lam_vec = jnp.stack([lambda_q1[l], lambda_k1[l], lambda_q2[l], lambda_k2[l]]).astype(F32)
        attn_out = _diff_attention(
            q, k, vt, lam_vec, attn_norm_g[l].reshape(n_heads, 1, V_HEAD_DIM),
            batch=batch, seq=seq, lam_init=lam_init, tq=512, heads=4)
        x2d = _ffn(
            x2d, pool_out, attn_out, w_o[l].astype(BF16), norm_ffn_g[l].reshape(1, d_model),
            w_up[l].astype(BF16), conv_w[l], conv_b[l].reshape(1, -1), w_down[l].astype(BF16),
            norm_final_g.reshape(1, d_model),
            seq=seq, tm=512, chunk=chunk, apply_final=(l == depth - 1))
    return x2d.reshape(batch, seq, d_model)
```

```python
import functools

import numpy as np
import jax
import jax.numpy as jnp
from jax import lax
from jax.experimental import pallas as pl
from jax.experimental.pallas import tpu as pltpu

CHUNK = 64
CHUNK_LOG2 = CHUNK.bit_length() - 1
POOL_WINDOWS = (2, 4, 8, 16)
POOL_GROUP_DIM = 128
DIFF_HEAD_DIM = 64
V_HEAD_DIM = 2 * DIFF_HEAD_DIM
CONV_WIDTH = 3
ROPE_THETA = 10000.0
EPS = 1e-6
LOG2_E = 1.4426950408889634

LANES = 128
SUBLANES = 8
MXU_DIM = 256

POOL_HALO = 16
CONV_HALO = SUBLANES
ONES_ROWS = 16
ACC_ROWS = V_HEAD_DIM + ONES_ROWS
MAX_WAYS = 8

BF16 = jnp.bfloat16
F32 = jnp.float32


def _dot(a, b):
    return jnp.dot(a, b, preferred_element_type=F32)


def _shift_rows(a, k):
    rows, width = a.shape
    a3 = a.reshape(rows // SUBLANES, SUBLANES, width)
    above = lambda x: jnp.concatenate([x[:1], x[:-1]], axis=0)
    if k == SUBLANES:
        return above(a3).reshape(rows, width)
    rot = pltpu.roll(a3, k, axis=1)
    sub = lax.broadcasted_iota(jnp.int32, (1, SUBLANES, width), 1)
    return jnp.where(sub < k, above(rot), rot).reshape(rows, width)


def _inproj_kernel(x_ref, pos_ref, invf_ref, g_ref, w_ref, pw_ref, ps_ref,
                   pool_ref, q_ref, k_ref, v_ref, carry_ref, *, tm, tiles_per_seq,
                   pool_width, attn_width):
    seq_tile = pl.program_id(0) % tiles_per_seq

    @pl.when(seq_tile == 0)
    def _():
        carry_ref[...] = jnp.zeros_like(carry_ref)

    x = x_ref[...]
    ms = jnp.mean(x * x, axis=-1, keepdims=True)
    h = ((x * lax.rsqrt(ms + EPS)) * g_ref[...]).astype(BF16)

    q0 = pool_width
    k0 = pool_width + attn_width
    v0 = pool_width + 2 * attn_width
    p = _dot(h, w_ref[:, 0:pool_width])
    q = _dot(h, w_ref[:, q0:k0])
    k = _dot(h, w_ref[:, k0:v0])
    v = _dot(h, w_ref[:, v0:v0 + attn_width])

    t = seq_tile * tm + lax.broadcasted_iota(jnp.int32, (tm, POOL_GROUP_DIM), 0)
    pooled_out = []
    for g, w in enumerate(POOL_WINDOWS):
        cols = slice(g * POOL_GROUP_DIM, (g + 1) * POOL_GROUP_DIM)
        pg = p[:, cols]
        s = jnp.concatenate([carry_ref[:, cols], pg], axis=0)
        span = 1
        while span < w:
            s = s + _shift_rows(s, span)
            span *= 2
        count = jnp.minimum(t + 1, w).astype(F32)
        d = (s[POOL_HALO:] / count - pg).astype(BF16)
        pooled_out.append(_dot(d, pw_ref[g]))
    carry_ref[...] = p[tm - POOL_HALO:, :]
    pool_ref[...] = (jnp.concatenate(pooled_out, axis=1) * ps_ref[...]).astype(pool_ref.dtype)

    half = DIFF_HEAD_DIM // 2
    groups = LANES // half
    tq = tm // groups
    ang = pos_ref[...] * invf_ref[...]
    cos_c = jnp.cos(ang)
    sin_c = jnp.sin(ang)
    lane = lax.broadcasted_iota(jnp.int32, (tq, LANES), 1)
    lane_group = lane // half
    first_half = (lane % DIFF_HEAD_DIM) < half
    sign = jnp.where(first_half, -1.0, 1.0).astype(F32)

    def spread(x, a):
        y = None
        for g in range(groups):
            shift = ((g - a) * half) % LANES
            xr = x if shift == 0 else pltpu.roll(x, shift, axis=1)
            y = xr if y is None else jnp.where(lane_group == g, xr, y)
        return y

    tables = [(spread(cos_c, a), spread(sin_c, a) * sign) for a in range(groups)]

    def rope(tt, scale):
        blocks = []
        for a, (cos, sin) in enumerate(tables):
            outs = []
            for j in range(attn_width // LANES):
                tj = tt[a * tq:(a + 1) * tq, j * LANES:(j + 1) * LANES]
                partner = jnp.where(first_half,
                                    pltpu.roll(tj, LANES - half, axis=1),
                                    pltpu.roll(tj, half, axis=1))
                outs.append((tj * cos + partner * sin) * scale)
            blocks.append(jnp.concatenate(outs, axis=1))
        return jnp.concatenate(blocks, axis=0)

    q_ref[...] = rope(q, DIFF_HEAD_DIM ** -0.5 * LOG2_E).astype(q_ref.dtype)
    k_ref[...] = rope(k, 1.0).astype(k_ref.dtype)
    v_ref[0] = v.T.astype(v_ref.dtype)


def _inproj(x2, pos_b, invf, g, w, pool_w, pool_scale, *, seq, tm):
    T, d_model = x2.shape
    pool_width = pool_scale.shape[-1]
    attn_width = (w.shape[1] - pool_width) // 3
    n_groups = pool_w.shape[0]
    kern = functools.partial(_inproj_kernel, tm=tm, tiles_per_seq=seq // tm,
                             pool_width=pool_width, attn_width=attn_width)
    row = lambda i: (i, 0)
    const2 = lambda i: (0, 0)
    out_sds = lambda width: jax.ShapeDtypeStruct((T, width), BF16)
    return pl.pallas_call(
        kern,
        grid=(T // tm,),
        in_specs=[
            pl.BlockSpec((tm, d_model), row),
            pl.BlockSpec((tm * (DIFF_HEAD_DIM // 2) // LANES, LANES), row),
            pl.BlockSpec((1, LANES), const2),
            pl.BlockSpec((1, d_model), const2),
            pl.BlockSpec(w.shape, const2),
            pl.BlockSpec(pool_w.shape, lambda i: (0, 0, 0)),
            pl.BlockSpec((1, pool_width), const2),
        ],
        out_specs=[
            pl.BlockSpec((tm, pool_width), row),
            pl.BlockSpec((tm, attn_width), row),
            pl.BlockSpec((tm, attn_width), row),
            pl.BlockSpec((1, attn_width, tm), lambda i: (i, 0, 0)),
        ],
        out_shape=[out_sds(pool_width), out_sds(attn_width), out_sds(attn_width),
                   jax.ShapeDtypeStruct((T // tm, attn_width, tm), BF16)],
        scratch_shapes=[pltpu.VMEM((POOL_HALO, n_groups * POOL_GROUP_DIM), F32)],
        compiler_params=pltpu.CompilerParams(
            dimension_semantics=("arbitrary",),
            vmem_limit_bytes=48 * 1024 * 1024),
        name="inproj_pool_rope",
    )(x2, pos_b, invf, g, w, pool_w, pool_scale)


def _attn_kernel(lam_ref, g_ref, q_ref, k_ref, vt_ref, o_ref, m_ref, acc_ref, sa_ref, sb_ref,
                 *, tq, tk, heads, lam_init):
    i = pl.program_id(2)
    cols = 2 * tq
    groups = tk // SUBLANES

    qqts = []

    def scores(hh, j):
        ch = slice(hh * V_HEAD_DIM, (hh + 1) * V_HEAD_DIM)
        kt = k_ref[pl.ds(pl.multiple_of(j * tk, tk), tk), ch]
        return _dot(kt, qqts[hh])

    def prologue():
        m_ref[...] = jnp.full(m_ref.shape, -jnp.inf, F32)
        acc_ref[...] = jnp.zeros(acc_ref.shape, F32)
        qqts.clear()
        sub = lax.broadcasted_iota(jnp.int32, (V_HEAD_DIM, tq), 0)
        for hh in range(heads):
            qt = q_ref[:, hh * V_HEAD_DIM:(hh + 1) * V_HEAD_DIM].astype(F32).T
            zero = jnp.zeros_like(qt)
            qqts.append(jnp.concatenate([jnp.where(sub < DIFF_HEAD_DIM, qt, zero),
                                         jnp.where(sub >= DIFF_HEAD_DIM, qt, zero)],
                                        axis=1).astype(BF16))
            sa_ref[hh] = scores(hh, 0)

    def softmax_pv(hh, j, s, masked):
        ch = slice(hh * V_HEAD_DIM, (hh + 1) * V_HEAD_DIM)
        if masked:
            c = lax.broadcasted_iota(jnp.int32, (SUBLANES, cols), 1)
            q_chunk = (i * tq - j * tk + jnp.where(c >= tq, c - tq, c)) >> CHUNK_LOG2
            blocks = []
            for kc in range(tk // CHUNK):
                blk = s[kc * CHUNK:(kc + 1) * CHUNK].reshape(CHUNK // SUBLANES, SUBLANES, cols)
                blocks.append(jnp.where((q_chunk >= kc)[None], blk, -jnp.inf).reshape(CHUNK, cols))
            s = jnp.concatenate(blocks, axis=0)
        s4 = s.reshape(MAX_WAYS, groups // MAX_WAYS, SUBLANES, cols)
        m_tile = jnp.max(jnp.max(jnp.max(s4, axis=1), axis=0), axis=0, keepdims=True)
        m_prev = m_ref[hh]
        m_new = jnp.maximum(m_prev, m_tile)
        alpha = jnp.exp2(m_prev - m_new)
        p = jnp.exp2(s.reshape(groups, SUBLANES, cols) - m_new[None]).reshape(tk, cols).astype(BF16)
        lhs = jnp.concatenate([vt_ref[j, ch, :], jnp.ones((ONES_ROWS, tk), BF16)], axis=0)
        pv = _dot(lhs, p)
        acc3 = acc_ref[hh].reshape(ACC_ROWS // SUBLANES, SUBLANES, cols)
        acc_ref[hh] = (alpha[None] * acc3).reshape(ACC_ROWS, cols) + pv
        m_ref[hh] = m_new

    def store_scores(j, buf):
        for hh in range(heads):
            buf[hh] = scores(hh, j)

    def full_step(j, cur, nxt):
        store_scores(j + 1, nxt)
        for hh in range(heads):
            softmax_pv(hh, j, cur[hh], masked=False)

    def last_step(j, cur):
        for hh in range(heads):
            softmax_pv(hh, j, cur[hh], masked=True)

    def finalize():
        lam_vec = lam_ref[...]
        lam = (jnp.exp(jnp.sum(lam_vec[0:1] * lam_vec[1:2], axis=1, keepdims=True))
               - jnp.exp(jnp.sum(lam_vec[2:3] * lam_vec[3:4], axis=1, keepdims=True))
               + lam_init)
        for hh in range(heads):
            acc = acc_ref[hh]
            ot = acc[:V_HEAD_DIM] / acc[V_HEAD_DIM:V_HEAD_DIM + 1]
            od = (ot[:, :tq] - lam * ot[:, tq:]).T
            y = od * lax.rsqrt(jnp.mean(od * od, axis=-1, keepdims=True) + EPS)
            o_ref[:, hh * V_HEAD_DIM:(hh + 1) * V_HEAD_DIM] = (
                (y * g_ref[hh]) * (1.0 - lam_init)).astype(o_ref.dtype)

    n_full = (i * tq) // tk

    @pl.when(n_full == 0)
    def _():
        prologue()
        last_step(0, sa_ref)
        finalize()

    @pl.when(n_full > 0)
    def _():
        prologue()
        full_step(0, sa_ref, sb_ref)
        rest = n_full - 1

        def pair(t, carry):
            full_step(2 * t + 1, sb_ref, sa_ref)
            full_step(2 * t + 2, sa_ref, sb_ref)
            return carry

        lax.fori_loop(0, rest // 2, pair, 0)

        @pl.when(rest % 2 == 1)
        def _():
            full_step(n_full - 1, sb_ref, sa_ref)
            last_step(n_full, sa_ref)
            finalize()

        @pl.when(rest % 2 == 0)
        def _():
            last_step(n_full, sb_ref)
            finalize()


def _diff_attention(q, k, vt, lam_vec, head_g, *, batch, seq, lam_init, tq, heads):
    T, attn_width = q.shape
    tk = vt.shape[2]
    n_heads = attn_width // V_HEAD_DIM
    nq = seq // tq
    nk = seq // tk
    width = heads * V_HEAD_DIM
    kern = functools.partial(_attn_kernel, tq=tq, tk=tk, heads=heads, lam_init=lam_init)
    q_spec = pl.BlockSpec((tq, width), lambda b, h, i: (b * nq + i, h))
    return pl.pallas_call(
        kern,
        grid=(batch, n_heads // heads, nq),
        in_specs=[
            pl.BlockSpec(lam_vec.shape, lambda b, h, i: (0, 0)),
            pl.BlockSpec((heads, 1, V_HEAD_DIM), lambda b, h, i: (h, 0, 0)),
            q_spec,
            pl.BlockSpec((seq, width), lambda b, h, i: (b, h)),
            pl.BlockSpec((nk, width, tk), lambda b, h, i: (b, h, 0)),
        ],
        out_specs=q_spec,
        out_shape=jax.ShapeDtypeStruct((T, attn_width), BF16),
        scratch_shapes=[
            pltpu.VMEM((heads, SUBLANES, 2 * tq), F32),
            pltpu.VMEM((heads, ACC_ROWS, 2 * tq), F32),
            pltpu.VMEM((heads, tk, 2 * tq), F32),
            pltpu.VMEM((heads, tk, 2 * tq), F32),
        ],
        compiler_params=pltpu.CompilerParams(
            dimension_semantics=("arbitrary", "arbitrary", "arbitrary"),
            vmem_limit_bytes=52 * 1024 * 1024),
        name="diff_attention",
    )(lam_vec, head_g, q, k, vt)


def _ffn_kernel(x_ref, pool_ref, attn_ref, wo_ref, gf_ref, wup_ref, cw_ref, cb_ref, wd_ref,
                gl_ref, o_ref, carry_ref, x1_ref, h_ref, a_ref, u00_ref, u01_ref, u10_ref, u11_ref,
                *, tm, slab, chunk, tiles_per_seq, n_chunks, apply_final):
    u_refs = ((u00_ref, u01_ref), (u10_ref, u11_ref))
    seq_tile = pl.program_id(0) % tiles_per_seq

    @pl.when(seq_tile == 0)
    def _():
        carry_ref[...] = jnp.zeros_like(carry_ref)

    mixed = jnp.concatenate([pool_ref[...], attn_ref[...]], axis=1)
    x1 = x_ref[...] + _dot(mixed, wo_ref[...])
    x1_ref[...] = x1
    ms = jnp.mean(x1 * x1, axis=-1, keepdims=True)
    h_ref[...] = ((x1 * lax.rsqrt(ms + EPS)) * gf_ref[...]).astype(BF16)

    n_slabs = tm // slab
    cols = lambda idx: slice(idx * chunk, (idx + 1) * chunk)

    def up_matmul(c, r):
        h = h_ref[r * slab:(r + 1) * slab, :]
        return [_dot(h, wup_ref[:, cols(idx)]) for idx in (c, n_chunks + c)]

    def up_store(c, slot, r, us):
        for br, idx in enumerate((c, n_chunks + c)):
            u_ref = u_refs[slot][br]
            if r == 0:
                u_ref[0:CONV_HALO, :] = carry_ref[:, cols(idx)]
            u_ref[CONV_HALO + r * slab:CONV_HALO + (r + 1) * slab, :] = us[br]
            if r == n_slabs - 1:
                carry_ref[:, cols(idx)] = u_ref[tm:, :]

    def conv_branch(slot, br, idx, r):
        r0 = CONV_HALO + r * slab
        u_ref = u_refs[slot][br]
        ext = u_ref[r0 - SUBLANES:r0 + slab, :].reshape(slab // SUBLANES + 1, SUBLANES, chunk)
        sub = lax.broadcasted_iota(jnp.int32, (1, SUBLANES, chunk), 1)

        def delay_one(a):
            rot = pltpu.roll(a, 1, axis=1)
            return jnp.concatenate([rot[:1], jnp.where(sub < 1, rot[:-1], rot[1:])], axis=0)

        d1 = delay_one(ext)
        d2 = delay_one(d1)
        u = ext[1:].reshape(slab, chunk)
        u1 = d1[1:].reshape(slab, chunk)
        u2 = d2[1:].reshape(slab, chunk)
        cw = cw_ref[:, cols(idx)]
        return ((cb_ref[:, cols(idx)] + cw[0:1] * u2) + cw[1:2] * u1) + cw[2:3] * u

    for r in range(n_slabs):
        up_store(0, 0, r, up_matmul(0, r))
    for c in range(n_chunks):
        slot = c % 2
        for r in range(n_slabs):
            nxt = up_matmul(c + 1, r) if c + 1 < n_chunks else None
            gate = conv_branch(slot, 0, c, r)
            val = conv_branch(slot, 1, n_chunks + c, r)
            act = (gate * jax.nn.sigmoid(gate)) * val
            a_ref[r * slab:(r + 1) * slab, cols(c)] = act.astype(BF16)
            if nxt is not None:
                up_store(c + 1, 1 - slot, r, nxt)
    ffn_out = _dot(a_ref[...], wd_ref[...])

    x2 = x1_ref[...] + ffn_out
    if apply_final:
        ms2 = jnp.mean(x2 * x2, axis=-1, keepdims=True)
        x2 = (x2 * lax.rsqrt(ms2 + EPS)) * gl_ref[...]
    o_ref[...] = x2


def _ffn(x2d, pool_out, attn_out, wo, g_ffn, wup, cw, cb, wd, g_final,
         *, seq, tm, chunk, apply_final):
    T, d_model = x2d.shape
    n_chunks = wd.shape[0] // chunk
    kern = functools.partial(_ffn_kernel, tm=tm, tiles_per_seq=seq // tm,
                             slab=MXU_DIM, chunk=chunk, n_chunks=n_chunks, apply_final=apply_final)
    row = lambda i: (i, 0)
    const2 = lambda i: (0, 0)
    resident = dict(pipeline_mode=pl.Buffered(1))
    return pl.pallas_call(
        kern,
        grid=(T // tm,),
        in_specs=[
            pl.BlockSpec((tm, d_model), row),
            pl.BlockSpec((tm, pool_out.shape[1]), row),
            pl.BlockSpec((tm, attn_out.shape[1]), row),
            pl.BlockSpec(wo.shape, const2, **resident),
            pl.BlockSpec((1, d_model), const2),
            pl.BlockSpec(wup.shape, const2, **resident),
            pl.BlockSpec(cw.shape, const2),
            pl.BlockSpec(cb.shape, const2),
            pl.BlockSpec(wd.shape, const2, **resident),
            pl.BlockSpec((1, d_model), const2),
        ],
        out_specs=pl.BlockSpec((tm, d_model), row),
        out_shape=jax.ShapeDtypeStruct((T, d_model), F32),
        scratch_shapes=[
            pltpu.VMEM((CONV_HALO, 2 * n_chunks * chunk), F32),
            pltpu.VMEM((tm, d_model), F32),
            pltpu.VMEM((tm, d_model), BF16),
            pltpu.VMEM((tm, n_chunks * chunk), BF16),
        ] + [pltpu.VMEM((CONV_HALO + tm, chunk), F32)] * 4,
        compiler_params=pltpu.CompilerParams(
            dimension_semantics=("arbitrary",),
            vmem_limit_bytes=56 * 1024 * 1024),
        name="outproj_convffn",
    )(x2d, pool_out, attn_out, wo, g_ffn, wup, cw, cb, wd, g_final)


def kernel(x, positions, norm_mix_g, w_in, pool_w, pool_scale, lambda_q1, lambda_k1, lambda_q2,
           lambda_k2, attn_norm_g, w_o, norm_ffn_g, w_up, conv_w, conv_b, w_down, norm_final_g):
    batch, seq, d_model = x.shape
    depth = w_in.shape[0]
    T = batch * seq
    d_ff = w_down.shape[1]
    n_heads = attn_norm_g.shape[1]
    chunk = MXU_DIM
    assert d_ff % chunk == 0 and seq % 512 == 0

    x2d = x.reshape(T, d_model)
    tm_in = 512
    half = DIFF_HEAD_DIM // 2
    groups = LANES // half
    pos_c = positions.astype(F32).reshape(T // tm_in, groups, tm_in // groups).transpose(0, 2, 1)
    pos_c = jnp.broadcast_to(pos_c[..., None], pos_c.shape + (half,)).reshape(T // groups, LANES)
    inv_freq = ROPE_THETA ** (-jnp.arange(0, DIFF_HEAD_DIM, 2, dtype=F32) / DIFF_HEAD_DIM)
    invf = jnp.tile(inv_freq, groups).reshape(1, LANES)

    for l in range(depth):
        lam_init = 0.8 - 0.6 * float(np.exp(-0.3 * l))
        pool_out, q, k, vt = _inproj(
            x2d, pos_c, invf, norm_mix_g[l].reshape(1, d_model), w_in[l].astype(BF16),
            pool_w[l].astype(BF16), pool_scale[l].reshape(1, -1), seq=seq, tm=tm_in)
        lam_vec = jnp.stack([lambda_q1[l], lambda_k1[l], lambda_q2[l], lambda_k2[l]]).astype(F32)
        attn_out = _diff_attention(
            q, k, vt, lam_vec, attn_norm_g[l].reshape(n_heads, 1, V_HEAD_DIM),
            batch=batch, seq=seq, lam_init=lam_init, tq=512, heads=4)
        x2d = _ffn(
            x2d, pool_out, attn_out, w_o[l].astype(BF16), norm_ffn_g[l].reshape(1, d_model),
            w_up[l].astype(BF16), conv_w[l], conv_b[l].reshape(1, -1), w_down[l].astype(BF16),
            norm_final_g.reshape(1, d_model),
            seq=seq, tm=512, chunk=chunk, apply_final=(l == depth - 1))
    return x2d.reshape(batch, seq, d_model)
```

```python
import functools

import numpy as np
import jax
import jax.numpy as jnp
from jax import lax
from jax.experimental import pallas as pl
from jax.experimental.pallas import tpu as pltpu

CHUNK = 64
POOL_WINDOWS = (2, 4, 8, 16)
POOL_GROUP_DIM = 128
DIFF_HEAD_DIM = 64
V_HEAD_DIM = 2 * DIFF_HEAD_DIM
CONV_WIDTH = 3
ROPE_THETA = 10000.0
EPS = 1e-6
LOG2_E = 1.4426950408889634

LANES = 128
SUBLANES = 8
MXU_DIM = 256

POOL_HALO = 16
CONV_HALO = SUBLANES
ONES_ROWS = 16
ACC_ROWS = V_HEAD_DIM + ONES_ROWS
MAX_WAYS = 8

ROW_TILE = 512

MIB = 1024 * 1024
INPROJ_VMEM_LIMIT = 48 * MIB
ATTN_VMEM_LIMIT = 52 * MIB
FFN_VMEM_LIMIT = 56 * MIB

BF16 = jnp.bfloat16
F32 = jnp.float32


def _dot(a, b):
    return jnp.dot(a, b, preferred_element_type=F32)


def _shift_rows(a, k):
    rows, width = a.shape
    a3 = a.reshape(rows // SUBLANES, SUBLANES, width)
    above = lambda x: jnp.concatenate([x[:1], x[:-1]], axis=0)
    if k == SUBLANES:
        return above(a3).reshape(rows, width)
    rot = pltpu.roll(a3, k, axis=1)
    sub = lax.broadcasted_iota(jnp.int32, (1, SUBLANES, width), 1)
    return jnp.where(sub < k, above(rot), rot).reshape(rows, width)


def _inproj_kernel(x_ref, pos_ref, invf_ref, g_ref, w_ref, pw_ref, ps_ref,
                   pool_ref, q_ref, k_ref, v_ref, carry_ref, *, tm, tiles_per_seq,
                   pool_width, attn_width):
    seq_tile = pl.program_id(0) % tiles_per_seq

    @pl.when(seq_tile == 0)
    def _():
        carry_ref[...] = jnp.zeros_like(carry_ref)

    x = x_ref[...]
    ms = jnp.mean(x * x, axis=-1, keepdims=True)
    h = ((x * lax.rsqrt(ms + EPS)) * g_ref[...]).astype(BF16)

    q0 = pool_width
    k0 = pool_width + attn_width
    v0 = pool_width + 2 * attn_width
    p = _dot(h, w_ref[:, 0:pool_width])
    q = _dot(h, w_ref[:, q0:k0])
    k = _dot(h, w_ref[:, k0:v0])
    v = _dot(h, w_ref[:, v0:v0 + attn_width])

    t = seq_tile * tm + lax.broadcasted_iota(jnp.int32, (tm, POOL_GROUP_DIM), 0)
    pooled_out = []
    for g, w in enumerate(POOL_WINDOWS):
        cols = slice(g * POOL_GROUP_DIM, (g + 1) * POOL_GROUP_DIM)
        pg = p[:, cols]
        s = jnp.concatenate([carry_ref[:, cols], pg], axis=0)
        span = 1
        while span < w:
            s = s + _shift_rows(s, span)
            span *= 2
        count = jnp.minimum(t + 1, w).astype(F32)
        d = (s[POOL_HALO:] / count - pg).astype(BF16)
        pooled_out.append(_dot(d, pw_ref[g]))
    carry_ref[...] = p[tm - POOL_HALO:, :]
    pool_ref[...] = (jnp.concatenate(pooled_out, axis=1) * ps_ref[...]).astype(pool_ref.dtype)

    half = DIFF_HEAD_DIM // 2
    groups = LANES // half
    tq = tm // groups
    ang = pos_ref[...] * invf_ref[...]
    cos_c = jnp.cos(ang)
    sin_c = jnp.sin(ang)
    lane = lax.broadcasted_iota(jnp.int32, (tq, LANES), 1)
    lane_group = lane // half
    first_half = (lane % DIFF_HEAD_DIM) < half
    sign = jnp.where(first_half, -1.0, 1.0).astype(F32)

    def spread(x, a):
        y = None
        for g in range(groups):
            shift = ((g - a) * half) % LANES
            xr = x if shift == 0 else pltpu.roll(x, shift, axis=1)
            y = xr if y is None else jnp.where(lane_group == g, xr, y)
        return y

    tables = [(spread(cos_c, a), spread(sin_c, a) * sign) for a in range(groups)]

    def rope(tt, scale):
        blocks = []
        for a, (cos, sin) in enumerate(tables):
            outs = []
            for j in range(attn_width // LANES):
                tj = tt[a * tq:(a + 1) * tq, j * LANES:(j + 1) * LANES]
                partner = jnp.where(first_half,
                                    pltpu.roll(tj, LANES - half, axis=1),
                                    pltpu.roll(tj, half, axis=1))
                outs.append((tj * cos + partner * sin) * scale)
            blocks.append(jnp.concatenate(outs, axis=1))
        return jnp.concatenate(blocks, axis=0)

    q_ref[...] = rope(q, DIFF_HEAD_DIM ** -0.5 * LOG2_E).astype(q_ref.dtype)
    k_ref[...] = rope(k, 1.0).astype(k_ref.dtype)
    v_ref[0] = v.T.astype(v_ref.dtype)


def _inproj(x2, pos_b, invf, g, w, pool_w, pool_scale, *, seq, tm):
    T, d_model = x2.shape
    pool_width = pool_scale.shape[-1]
    attn_width = (w.shape[1] - pool_width) // 3
    n_groups = pool_w.shape[0]
    kern = functools.partial(_inproj_kernel, tm=tm, tiles_per_seq=seq // tm,
                             pool_width=pool_width, attn_width=attn_width)
    row = lambda i: (i, 0)
    const2 = lambda i: (0, 0)
    out_sds = lambda width: jax.ShapeDtypeStruct((T, width), BF16)
    return pl.pallas_call(
        kern,
        grid=(T // tm,),
        in_specs=[
            pl.BlockSpec((tm, d_model), row),
            pl.BlockSpec((tm * (DIFF_HEAD_DIM // 2) // LANES, LANES), row),
            pl.BlockSpec((1, LANES), const2),
            pl.BlockSpec((1, d_model), const2),
            pl.BlockSpec(w.shape, const2),
            pl.BlockSpec(pool_w.shape, lambda i: (0, 0, 0)),
            pl.BlockSpec((1, pool_width), const2),
        ],
        out_specs=[
            pl.BlockSpec((tm, pool_width), row),
            pl.BlockSpec((tm, attn_width), row),
            pl.BlockSpec((tm, attn_width), row),
            pl.BlockSpec((1, attn_width, tm), lambda i: (i, 0, 0)),
        ],
        out_shape=[out_sds(pool_width), out_sds(attn_width), out_sds(attn_width),
                   jax.ShapeDtypeStruct((T // tm, attn_width, tm), BF16)],
        scratch_shapes=[pltpu.VMEM((POOL_HALO, n_groups * POOL_GROUP_DIM), F32)],
        compiler_params=pltpu.CompilerParams(
            dimension_semantics=("arbitrary",),
            vmem_limit_bytes=INPROJ_VMEM_LIMIT),
        name="inproj_pool_rope",
    )(x2, pos_b, invf, g, w, pool_w, pool_scale)


def _attn_kernel(lam_ref, g_ref, q_ref, k_ref, vt_ref, o_ref, m_ref, acc_ref, sa_ref, sb_ref,
                 *, tq, tk, heads, lam_init):
    i = pl.program_id(2)
    cols = 2 * tq
    groups = tk // SUBLANES

    qqts = []

    def scores(hh, j):
        ch = slice(hh * V_HEAD_DIM, (hh + 1) * V_HEAD_DIM)
        kt = k_ref[pl.ds(pl.multiple_of(j * tk, tk), tk), ch]
        return _dot(kt, qqts[hh])

    def prologue():
        m_ref[...] = jnp.full(m_ref.shape, -jnp.inf, F32)
        acc_ref[...] = jnp.zeros(acc_ref.shape, F32)
        qqts.clear()
        sub = lax.broadcasted_iota(jnp.int32, (V_HEAD_DIM, tq), 0)
        for hh in range(heads):
            qt = q_ref[:, hh * V_HEAD_DIM:(hh + 1) * V_HEAD_DIM].astype(F32).T
            zero = jnp.zeros_like(qt)
            qqts.append(jnp.concatenate([jnp.where(sub < DIFF_HEAD_DIM, qt, zero),
                                         jnp.where(sub >= DIFF_HEAD_DIM, qt, zero)],
                                        axis=1).astype(BF16))
            sa_ref[hh] = scores(hh, 0)

    def softmax_pv(hh, j, s, masked):
        ch = slice(hh * V_HEAD_DIM, (hh + 1) * V_HEAD_DIM)
        m_prev = m_ref[hh]
        if masked:
            lane = lax.broadcasted_iota(jnp.int32, (CHUNK, LANES), 1)
            m_blocks, p_blocks = [], []
            for blk_i in range(cols // LANES):
                cs = slice(blk_i * LANES, (blk_i + 1) * LANES)
                live = CHUNK * (2 * (blk_i % (tq // LANES)) + 2)
                body = s[0:live - CHUNK, cs]
                tail = jnp.where(lane < CHUNK, -jnp.inf, s[live - CHUNK:live, cs])
                sb = jnp.concatenate([body, tail], axis=0).reshape(live // SUBLANES, SUBLANES, LANES)
                m_blk = jnp.maximum(m_prev[:, cs], jnp.max(jnp.max(sb, axis=0), axis=0, keepdims=True))
                p_blk = jnp.exp2(sb - m_blk[None]).reshape(live, LANES).astype(BF16)
                if live < tk:
                    p_blk = jnp.concatenate([p_blk, jnp.zeros((tk - live, LANES), BF16)], axis=0)
                m_blocks.append(m_blk)
                p_blocks.append(p_blk)
            m_new = jnp.concatenate(m_blocks, axis=1)
            p = jnp.concatenate(p_blocks, axis=1)
        else:
            s4 = s.reshape(MAX_WAYS, groups // MAX_WAYS, SUBLANES, cols)
            m_tile = jnp.max(jnp.max(jnp.max(s4, axis=1), axis=0), axis=0, keepdims=True)
            m_new = jnp.maximum(m_prev, m_tile)
            p = jnp.exp2(s.reshape(groups, SUBLANES, cols) - m_new[None]).reshape(tk, cols).astype(BF16)
        alpha = jnp.exp2(m_prev - m_new)
        lhs = jnp.concatenate([vt_ref[j, ch, :], jnp.ones((ONES_ROWS, tk), BF16)], axis=0)
        pv = _dot(lhs, p)
        acc3 = acc_ref[hh].reshape(ACC_ROWS // SUBLANES, SUBLANES, cols)
        acc_ref[hh] = (alpha[None] * acc3).reshape(ACC_ROWS, cols) + pv
        m_ref[hh] = m_new

    def store_scores(j, buf):
        for hh in range(heads):
            buf[hh] = scores(hh, j)

    def full_step(j, cur, nxt):
        store_scores(j + 1, nxt)
        for hh in range(heads):
            softmax_pv(hh, j, cur[hh], masked=False)

    def last_step(j, cur):
        for hh in range(heads):
            softmax_pv(hh, j, cur[hh], masked=True)

    def finalize():
        lam_vec = lam_ref[...]
        lam = (jnp.exp(jnp.sum(lam_vec[0:1] * lam_vec[1:2], axis=1, keepdims=True))
               - jnp.exp(jnp.sum(lam_vec[2:3] * lam_vec[3:4], axis=1, keepdims=True))
               + lam_init)
        for hh in range(heads):
            acc = acc_ref[hh]
            ot = acc[:V_HEAD_DIM] / acc[V_HEAD_DIM:V_HEAD_DIM + 1]
            od = (ot[:, :tq] - lam * ot[:, tq:]).T
            y = od * lax.rsqrt(jnp.mean(od * od, axis=-1, keepdims=True) + EPS)
            o_ref[:, hh * V_HEAD_DIM:(hh + 1) * V_HEAD_DIM] = (
                (y * g_ref[hh]) * (1.0 - lam_init)).astype(o_ref.dtype)

    n_full = (i * tq) // tk

    @pl.when(n_full == 0)
    def _():
        prologue()
        last_step(0, sa_ref)
        finalize()

    @pl.when(n_full > 0)
    def _():
        prologue()
        full_step(0, sa_ref, sb_ref)
        rest = n_full - 1

        def pair(t, carry):
            full_step(2 * t + 1, sb_ref, sa_ref)
            full_step(2 * t + 2, sa_ref, sb_ref)
            return carry

        lax.fori_loop(0, rest // 2, pair, 0)

        @pl.when(rest % 2 == 1)
        def _():
            full_step(n_full - 1, sb_ref, sa_ref)
            last_step(n_full, sa_ref)
            finalize()

        @pl.when(rest % 2 == 0)
        def _():
            last_step(n_full, sb_ref)
            finalize()


def _diff_attention(q, k, vt, lam_vec, head_g, *, batch, seq, lam_init, tq, heads):
    T, attn_width = q.shape
    tk = vt.shape[2]
    assert tq == tk and 2 * CHUNK == LANES and tq % LANES == 0
    n_heads = attn_width // V_HEAD_DIM
    nq = seq // tq
    nk = seq // tk
    width = heads * V_HEAD_DIM
    kern = functools.partial(_attn_kernel, tq=tq, tk=tk, heads=heads, lam_init=lam_init)
    q_spec = pl.BlockSpec((tq, width), lambda b, h, i: (b * nq + i, h))
    return pl.pallas_call(
        kern,
        grid=(batch, n_heads // heads, nq),
        in_specs=[
            pl.BlockSpec(lam_vec.shape, lambda b, h, i: (0, 0)),
            pl.BlockSpec((heads, 1, V_HEAD_DIM), lambda b, h, i: (h, 0, 0)),
            q_spec,
            pl.BlockSpec((seq, width), lambda b, h, i: (b, h)),
            pl.BlockSpec((nk, width, tk), lambda b, h, i: (b, h, 0)),
        ],
        out_specs=q_spec,
        out_shape=jax.ShapeDtypeStruct((T, attn_width), BF16),
        scratch_shapes=[
            pltpu.VMEM((heads, SUBLANES, 2 * tq), F32),
            pltpu.VMEM((heads, ACC_ROWS, 2 * tq), F32),
            pltpu.VMEM((heads, tk, 2 * tq), F32),
            pltpu.VMEM((heads, tk, 2 * tq), F32),
        ],
        compiler_params=pltpu.CompilerParams(
            dimension_semantics=("arbitrary", "arbitrary", "arbitrary"),
            vmem_limit_bytes=ATTN_VMEM_LIMIT),
        name="diff_attention",
    )(lam_vec, head_g, q, k, vt)


def _ffn_kernel(x_ref, pool_ref, attn_ref, wo_ref, gf_ref, wup_ref, cw_ref, cb_ref, wd_ref,
                gl_ref, o_ref, carry_ref, x1_ref, h_ref, a_ref, u00_ref, u01_ref, u10_ref, u11_ref,
                *, tm, slab, chunk, tiles_per_seq, n_chunks, apply_final):
    u_refs = ((u00_ref, u01_ref), (u10_ref, u11_ref))
    seq_tile = pl.program_id(0) % tiles_per_seq

    @pl.when(seq_tile == 0)
    def _():
        carry_ref[...] = jnp.zeros_like(carry_ref)

    mixed = jnp.concatenate([pool_ref[...], attn_ref[...]], axis=1)
    x1 = x_ref[...] + _dot(mixed, wo_ref[...])
    x1_ref[...] = x1
    ms = jnp.mean(x1 * x1, axis=-1, keepdims=True)
    h_ref[...] = ((x1 * lax.rsqrt(ms + EPS)) * gf_ref[...]).astype(BF16)

    n_slabs = tm // slab
    cols = lambda idx: slice(idx * chunk, (idx + 1) * chunk)

    def up_matmul(c, r):
        h = h_ref[r * slab:(r + 1) * slab, :]
        return [_dot(h, wup_ref[:, cols(idx)]) for idx in (c, n_chunks + c)]

    def up_store(c, slot, r, us):
        for br, idx in enumerate((c, n_chunks + c)):
            u_ref = u_refs[slot][br]
            if r == 0:
                u_ref[0:CONV_HALO, :] = carry_ref[:, cols(idx)]
            u_ref[CONV_HALO + r * slab:CONV_HALO + (r + 1) * slab, :] = us[br]
            if r == n_slabs - 1:
                carry_ref[:, cols(idx)] = u_ref[tm:, :]

    def conv_branch(slot, br, idx, r):
        r0 = CONV_HALO + r * slab
        u_ref = u_refs[slot][br]
        ext = u_ref[r0 - SUBLANES:r0 + slab, :].reshape(slab // SUBLANES + 1, SUBLANES, chunk)
        sub = lax.broadcasted_iota(jnp.int32, (1, SUBLANES, chunk), 1)

        def delay_one(a):
            rot = pltpu.roll(a, 1, axis=1)
            return jnp.concatenate([rot[:1], jnp.where(sub < 1, rot[:-1], rot[1:])], axis=0)

        d1 = delay_one(ext)
        d2 = delay_one(d1)
        u = ext[1:].reshape(slab, chunk)
        u1 = d1[1:].reshape(slab, chunk)
        u2 = d2[1:].reshape(slab, chunk)
        cw = cw_ref[:, cols(idx)]
        return ((cb_ref[:, cols(idx)] + cw[0:1] * u2) + cw[1:2] * u1) + cw[2:3] * u

    for r in range(n_slabs):
        up_store(0, 0, r, up_matmul(0, r))
    for c in range(n_chunks):
        slot = c % 2
        for r in range(n_slabs):
            nxt = up_matmul(c + 1, r) if c + 1 < n_chunks else None
            gate = conv_branch(slot, 0, c, r)
            val = conv_branch(slot, 1, n_chunks + c, r)
            act = (gate * jax.nn.sigmoid(gate)) * val
            a_ref[r * slab:(r + 1) * slab, cols(c)] = act.astype(BF16)
            if nxt is not None:
                up_store(c + 1, 1 - slot, r, nxt)
    ffn_out = _dot(a_ref[...], wd_ref[...])

    x2 = x1_ref[...] + ffn_out
    if apply_final:
        ms2 = jnp.mean(x2 * x2, axis=-1, keepdims=True)
        x2 = (x2 * lax.rsqrt(ms2 + EPS)) * gl_ref[...]
    o_ref[...] = x2


def _ffn(x2d, pool_out, attn_out, wo, g_ffn, wup, cw, cb, wd, g_final,
         *, seq, tm, chunk, apply_final):
    T, d_model = x2d.shape
    n_chunks = wd.shape[0] // chunk
    kern = functools.partial(_ffn_kernel, tm=tm, tiles_per_seq=seq // tm,
                             slab=MXU_DIM, chunk=chunk, n_chunks=n_chunks, apply_final=apply_final)
    row = lambda i: (i, 0)
    const2 = lambda i: (0, 0)
    resident = dict(pipeline_mode=pl.Buffered(1))
    return pl.pallas_call(
        kern,
        grid=(T // tm,),
        in_specs=[
            pl.BlockSpec((tm, d_model), row),
            pl.BlockSpec((tm, pool_out.shape[1]), row),
            pl.BlockSpec((tm, attn_out.shape[1]), row),
            pl.BlockSpec(wo.shape, const2, **resident),
            pl.BlockSpec((1, d_model), const2),
            pl.BlockSpec(wup.shape, const2, **resident),
            pl.BlockSpec(cw.shape, const2),
            pl.BlockSpec(cb.shape, const2),
            pl.BlockSpec(wd.shape, const2, **resident),
            pl.BlockSpec((1, d_model), const2),
        ],
        out_specs=pl.BlockSpec((tm, d_model), row),
        out_shape=jax.ShapeDtypeStruct((T, d_model), F32),
        scratch_shapes=[
            pltpu.VMEM((CONV_HALO, 2 * n_chunks * chunk), F32),
            pltpu.VMEM((tm, d_model), F32),
            pltpu.VMEM((tm, d_model), BF16),
            pltpu.VMEM((tm, n_chunks * chunk), BF16),
        ] + [pltpu.VMEM((CONV_HALO + tm, chunk), F32)] * 4,
        compiler_params=pltpu.CompilerParams(
            dimension_semantics=("arbitrary",),
            vmem_limit_bytes=FFN_VMEM_LIMIT),
        name="outproj_convffn",
    )(x2d, pool_out, attn_out, wo, g_ffn, wup, cw, cb, wd, g_final)


def kernel(x, positions, norm_mix_g, w_in, pool_w, pool_scale, lambda_q1, lambda_k1, lambda_q2,
           lambda_k2, attn_norm_g, w_o, norm_ffn_g, w_up, conv_w, conv_b, w_down, norm_final_g):
    batch, seq, d_model = x.shape
    depth = w_in.shape[0]
    T = batch * seq
    d_ff = w_down.shape[1]
    n_heads = attn_norm_g.shape[1]
    chunk = MXU_DIM
    assert d_ff % chunk == 0 and seq % ROW_TILE == 0

    x2d = x.reshape(T, d_model)
    half = DIFF_HEAD_DIM // 2
    groups = LANES // half
    pos_c = positions.astype(F32).reshape(T // ROW_TILE, groups, ROW_TILE // groups).transpose(0, 2, 1)
    pos_c = jnp.broadcast_to(pos_c[..., None], pos_c.shape + (half,)).reshape(T // groups, LANES)
    inv_freq = ROPE_THETA ** (-jnp.arange(0, DIFF_HEAD_DIM, 2, dtype=F32) / DIFF_HEAD_DIM)
    invf = jnp.tile(inv_freq, groups).reshape(1, LANES)

    for l in range(depth):
        lam_init = 0.8 - 0.6 * float(np.exp(-0.3 * l))
        pool_out, q, k, vt = _inproj(
            x2d, pos_c, invf, norm_mix_g[l].reshape(1, d_model), w_in[l].astype(BF16),
            pool_w[l].astype(BF16), pool_scale[l].reshape(1, -1), seq=seq, tm=ROW_TILE)
        lam_vec = jnp.stack([lambda_q1[l], lambda_k1[l], lambda_q2[l], lambda_k2[l]]).astype(F32)
        attn_out = _diff_attention(
            q, k, vt, lam_vec, attn_norm_g[l].reshape(n_heads, 1, V_HEAD_DIM),
            batch=batch, seq=seq, lam_init=lam_init, tq=ROW_TILE, heads=n_heads)
        x2d = _ffn(
            x2d, pool_out, attn_out, w_o[l].astype(BF16), norm_ffn_g[l].reshape(1, d_model),
            w_up[l].astype(BF16), conv_w[l], conv_b[l].reshape(1, -1), w_down[l].astype(BF16),
            norm_final_g.reshape(1, d_model),
            seq=seq, tm=ROW_TILE, chunk=chunk, apply_final=(l == depth - 1))
    return x2d.reshape(batch, seq, d_model)
```

```python
import functools

import numpy as np
import jax
import jax.numpy as jnp
from jax import lax
from jax.experimental import pallas as pl
from jax.experimental.pallas import tpu as pltpu

CHUNK = 64
POOL_WINDOWS = (2, 4, 8, 16)
POOL_GROUP_DIM = 128
DIFF_HEAD_DIM = 64
V_HEAD_DIM = 2 * DIFF_HEAD_DIM
CONV_WIDTH = 3
ROPE_THETA = 10000.0
EPS = 1e-6
LOG2_E = 1.4426950408889634

LANES = 128
SUBLANES = 8
MXU_DIM = 256

POOL_HALO = 16
CONV_HALO = SUBLANES
ONES_ROWS = 16
ACC_ROWS = V_HEAD_DIM + ONES_ROWS
MAX_WAYS = 8

ROW_TILE = 512

MIB = 1024 * 1024
INPROJ_VMEM_LIMIT = 48 * MIB
ATTN_VMEM_LIMIT = 52 * MIB
FFN_VMEM_LIMIT = 56 * MIB

BF16 = jnp.bfloat16
F32 = jnp.float32


def _dot(a, b):
    return jnp.dot(a, b, preferred_element_type=F32)


def _shift_rows(a, k):
    rows, width = a.shape
    a3 = a.reshape(rows // SUBLANES, SUBLANES, width)
    above = lambda x: jnp.concatenate([x[:1], x[:-1]], axis=0)
    if k == SUBLANES:
        return above(a3).reshape(rows, width)
    rot = pltpu.roll(a3, k, axis=1)
    sub = lax.broadcasted_iota(jnp.int32, (1, SUBLANES, width), 1)
    return jnp.where(sub < k, above(rot), rot).reshape(rows, width)


def _inproj_kernel(x_ref, pos_ref, invf_ref, g_ref, w_ref, pw_ref, ps_ref,
                   pool_ref, q_ref, k_ref, v_ref, carry_ref, *, tm, tiles_per_seq,
                   pool_width, attn_width):
    seq_tile = pl.program_id(0) % tiles_per_seq

    @pl.when(seq_tile == 0)
    def _():
        carry_ref[...] = jnp.zeros_like(carry_ref)

    x = x_ref[...]
    ms = jnp.mean(x * x, axis=-1, keepdims=True)
    h = ((x * lax.rsqrt(ms + EPS)) * g_ref[...]).astype(BF16)

    q0 = pool_width
    k0 = pool_width + attn_width
    v0 = pool_width + 2 * attn_width
    p = _dot(h, w_ref[:, 0:pool_width])
    q = _dot(h, w_ref[:, q0:k0])
    k = _dot(h, w_ref[:, k0:v0])
    v = _dot(h, w_ref[:, v0:v0 + attn_width])

    t = seq_tile * tm + lax.broadcasted_iota(jnp.int32, (tm, POOL_GROUP_DIM), 0)
    pooled_out = []
    for g, w in enumerate(POOL_WINDOWS):
        cols = slice(g * POOL_GROUP_DIM, (g + 1) * POOL_GROUP_DIM)
        pg = p[:, cols]
        s = jnp.concatenate([carry_ref[:, cols], pg], axis=0)
        span = 1
        while span < w:
            s = s + _shift_rows(s, span)
            span *= 2
        count = jnp.minimum(t + 1, w).astype(F32)
        d = (s[POOL_HALO:] / count - pg).astype(BF16)
        pooled_out.append(_dot(d, pw_ref[g]))
    carry_ref[...] = p[tm - POOL_HALO:, :]
    pool_ref[...] = (jnp.concatenate(pooled_out, axis=1) * ps_ref[...]).astype(pool_ref.dtype)

    half = DIFF_HEAD_DIM // 2
    groups = LANES // half
    tq = tm // groups
    ang = pos_ref[...] * invf_ref[...]
    cos_c = jnp.cos(ang)
    sin_c = jnp.sin(ang)
    lane = lax.broadcasted_iota(jnp.int32, (tq, LANES), 1)
    lane_group = lane // half
    first_half = (lane % DIFF_HEAD_DIM) < half
    sign = jnp.where(first_half, -1.0, 1.0).astype(F32)

    def spread(x, a):
        y = None
        for g in range(groups):
            shift = ((g - a) * half) % LANES
            xr = x if shift == 0 else pltpu.roll(x, shift, axis=1)
            y = xr if y is None else jnp.where(lane_group == g, xr, y)
        return y

    tables = [(spread(cos_c, a), spread(sin_c, a) * sign) for a in range(groups)]

    def rope(tt, scale):
        blocks = []
        for a, (cos, sin) in enumerate(tables):
            outs = []
            for j in range(attn_width // LANES):
                tj = tt[a * tq:(a + 1) * tq, j * LANES:(j + 1) * LANES]
                partner = jnp.where(first_half,
                                    pltpu.roll(tj, LANES - half, axis=1),
                                    pltpu.roll(tj, half, axis=1))
                outs.append((tj * cos + partner * sin) * scale)
            blocks.append(jnp.concatenate(outs, axis=1))
        return jnp.concatenate(blocks, axis=0)

    q_ref[...] = rope(q, DIFF_HEAD_DIM ** -0.5 * LOG2_E).astype(q_ref.dtype)
    k_ref[...] = rope(k, 1.0).astype(k_ref.dtype)
    v_ref[0] = v.T.astype(v_ref.dtype)


def _inproj(x2, pos_b, invf, g, w, pool_w, pool_scale, *, seq, tm):
    T, d_model = x2.shape
    pool_width = pool_scale.shape[-1]
    attn_width = (w.shape[1] - pool_width) // 3
    n_groups = pool_w.shape[0]
    kern = functools.partial(_inproj_kernel, tm=tm, tiles_per_seq=seq // tm,
                             pool_width=pool_width, attn_width=attn_width)
    row = lambda i: (i, 0)
    const2 = lambda i: (0, 0)
    out_sds = lambda width: jax.ShapeDtypeStruct((T, width), BF16)
    return pl.pallas_call(
        kern,
        grid=(T // tm,),
        in_specs=[
            pl.BlockSpec((tm, d_model), row),
            pl.BlockSpec((tm * (DIFF_HEAD_DIM // 2) // LANES, LANES), row),
            pl.BlockSpec((1, LANES), const2),
            pl.BlockSpec((1, d_model), const2),
            pl.BlockSpec(w.shape, const2),
            pl.BlockSpec(pool_w.shape, lambda i: (0, 0, 0)),
            pl.BlockSpec((1, pool_width), const2),
        ],
        out_specs=[
            pl.BlockSpec((tm, pool_width), row),
            pl.BlockSpec((tm, attn_width), row),
            pl.BlockSpec((tm, attn_width), row),
            pl.BlockSpec((1, attn_width, tm), lambda i: (i, 0, 0)),
        ],
        out_shape=[out_sds(pool_width), out_sds(attn_width), out_sds(attn_width),
                   jax.ShapeDtypeStruct((T // tm, attn_width, tm), BF16)],
        scratch_shapes=[pltpu.VMEM((POOL_HALO, n_groups * POOL_GROUP_DIM), F32)],
        compiler_params=pltpu.CompilerParams(
            dimension_semantics=("arbitrary",),
            vmem_limit_bytes=INPROJ_VMEM_LIMIT),
        name="inproj_pool_rope",
    )(x2, pos_b, invf, g, w, pool_w, pool_scale)


def _attn_kernel(lam_ref, g_ref, q_ref, k_ref, vt_ref, o_ref, m_ref, acc_ref, sa_ref, sb_ref,
                 ma_ref, mb_ref,
                 *, tq, tk, heads, lam_init):
    i = pl.program_id(2)
    cols = 2 * tq
    groups = tk // SUBLANES

    qqts = []
    buf_a, buf_b = (sa_ref, ma_ref), (sb_ref, mb_ref)

    def put_scores(buf, hh, s):
        s4 = s.reshape(MAX_WAYS, groups // MAX_WAYS, SUBLANES, cols)
        buf[0][hh] = s
        buf[1][hh] = jnp.max(jnp.max(jnp.max(s4, axis=1), axis=0), axis=0, keepdims=True)

    def scores(hh, j):
        ch = slice(hh * V_HEAD_DIM, (hh + 1) * V_HEAD_DIM)
        kt = k_ref[pl.ds(pl.multiple_of(j * tk, tk), tk), ch]
        return _dot(kt, qqts[hh])

    def prologue():
        m_ref[...] = jnp.full(m_ref.shape, -jnp.inf, F32)
        acc_ref[...] = jnp.zeros(acc_ref.shape, F32)
        qqts.clear()
        sub = lax.broadcasted_iota(jnp.int32, (V_HEAD_DIM, tq), 0)
        for hh in range(heads):
            qt = q_ref[:, hh * V_HEAD_DIM:(hh + 1) * V_HEAD_DIM].astype(F32).T
            zero = jnp.zeros_like(qt)
            qqts.append(jnp.concatenate([jnp.where(sub < DIFF_HEAD_DIM, qt, zero),
                                         jnp.where(sub >= DIFF_HEAD_DIM, qt, zero)],
                                        axis=1).astype(BF16))
            put_scores(buf_a, hh, scores(hh, 0))

    def softmax_pv(hh, j, s, m_tile, masked):
        ch = slice(hh * V_HEAD_DIM, (hh + 1) * V_HEAD_DIM)
        m_prev = m_ref[hh]
        if masked:
            lane = lax.broadcasted_iota(jnp.int32, (CHUNK, LANES), 1)
            m_blocks, p_blocks = [], []
            for blk_i in range(cols // LANES):
                cs = slice(blk_i * LANES, (blk_i + 1) * LANES)
                live = CHUNK * (2 * (blk_i % (tq // LANES)) + 2)
                body = s[0:live - CHUNK, cs]
                tail = jnp.where(lane < CHUNK, -jnp.inf, s[live - CHUNK:live, cs])
                sb = jnp.concatenate([body, tail], axis=0).reshape(live // SUBLANES, SUBLANES, LANES)
                m_blk = jnp.maximum(m_prev[:, cs], jnp.max(jnp.max(sb, axis=0), axis=0, keepdims=True))
                p_blk = jnp.exp2(sb - m_blk[None]).reshape(live, LANES).astype(BF16)
                if live < tk:
                    p_blk = jnp.concatenate([p_blk, jnp.zeros((tk - live, LANES), BF16)], axis=0)
                m_blocks.append(m_blk)
                p_blocks.append(p_blk)
            m_new = jnp.concatenate(m_blocks, axis=1)
            p = jnp.concatenate(p_blocks, axis=1)
        else:
            m_new = jnp.maximum(m_prev, m_tile)
            p = jnp.exp2(s.reshape(groups, SUBLANES, cols) - m_new[None]).reshape(tk, cols).astype(BF16)
        alpha = jnp.exp2(m_prev - m_new)
        lhs = jnp.concatenate([vt_ref[j, ch, :], jnp.ones((ONES_ROWS, tk), BF16)], axis=0)
        pv = _dot(lhs, p)
        acc3 = acc_ref[hh].reshape(ACC_ROWS // SUBLANES, SUBLANES, cols)
        acc_ref[hh] = (alpha[None] * acc3).reshape(ACC_ROWS, cols) + pv
        m_ref[hh] = m_new

    def store_scores(j, buf):
        for hh in range(heads):
            put_scores(buf, hh, scores(hh, j))

    def full_step(j, cur, nxt):
        store_scores(j + 1, nxt)
        for hh in range(heads):
            softmax_pv(hh, j, cur[0][hh], cur[1][hh], masked=False)

    def last_step(j, cur):
        for hh in range(heads):
            softmax_pv(hh, j, cur[0][hh], None, masked=True)

    def finalize():
        lam_vec = lam_ref[...]
        lam = (jnp.exp(jnp.sum(lam_vec[0:1] * lam_vec[1:2], axis=1, keepdims=True))
               - jnp.exp(jnp.sum(lam_vec[2:3] * lam_vec[3:4], axis=1, keepdims=True))
               + lam_init)
        for hh in range(heads):
            acc = acc_ref[hh]
            ot = acc[:V_HEAD_DIM] / acc[V_HEAD_DIM:V_HEAD_DIM + 1]
            od = (ot[:, :tq] - lam * ot[:, tq:]).T
            y = od * lax.rsqrt(jnp.mean(od * od, axis=-1, keepdims=True) + EPS)
            o_ref[:, hh * V_HEAD_DIM:(hh + 1) * V_HEAD_DIM] = (
                (y * g_ref[hh]) * (1.0 - lam_init)).astype(o_ref.dtype)

    n_full = (i * tq) // tk

    @pl.when(n_full == 0)
    def _():
        prologue()
        last_step(0, buf_a)
        finalize()

    @pl.when(n_full > 0)
    def _():
        prologue()
        full_step(0, buf_a, buf_b)
        rest = n_full - 1

        def pair(t, carry):
            full_step(2 * t + 1, buf_b, buf_a)
            full_step(2 * t + 2, buf_a, buf_b)
            return carry

        lax.fori_loop(0, rest // 2, pair, 0)

        @pl.when(rest % 2 == 1)
        def _():
            full_step(n_full - 1, buf_b, buf_a)
            last_step(n_full, buf_a)
            finalize()

        @pl.when(rest % 2 == 0)
        def _():
            last_step(n_full, buf_b)
            finalize()


def _diff_attention(q, k, vt, lam_vec, head_g, *, batch, seq, lam_init, tq, heads):
    T, attn_width = q.shape
    tk = vt.shape[2]
    assert tq == tk and 2 * CHUNK == LANES and tq % LANES == 0
    n_heads = attn_width // V_HEAD_DIM
    nq = seq // tq
    nk = seq // tk
    width = heads * V_HEAD_DIM
    kern = functools.partial(_attn_kernel, tq=tq, tk=tk, heads=heads, lam_init=lam_init)
    q_spec = pl.BlockSpec((tq, width), lambda b, h, i: (b * nq + i, h))
    return pl.pallas_call(
        kern,
        grid=(batch, n_heads // heads, nq),
        in_specs=[
            pl.BlockSpec(lam_vec.shape, lambda b, h, i: (0, 0)),
            pl.BlockSpec((heads, 1, V_HEAD_DIM), lambda b, h, i: (h, 0, 0)),
            q_spec,
            pl.BlockSpec((seq, width), lambda b, h, i: (b, h)),
            pl.BlockSpec((nk, width, tk), lambda b, h, i: (b, h, 0)),
        ],
        out_specs=q_spec,
        out_shape=jax.ShapeDtypeStruct((T, attn_width), BF16),
        scratch_shapes=[
            pltpu.VMEM((heads, SUBLANES, 2 * tq), F32),
            pltpu.VMEM((heads, ACC_ROWS, 2 * tq), F32),
            pltpu.VMEM((heads, tk, 2 * tq), F32),
            pltpu.VMEM((heads, tk, 2 * tq), F32),
            pltpu.VMEM((heads, 1, 2 * tq), F32),
            pltpu.VMEM((heads, 1, 2 * tq), F32),
        ],
        compiler_params=pltpu.CompilerParams(
            dimension_semantics=("arbitrary", "arbitrary", "arbitrary"),
            vmem_limit_bytes=ATTN_VMEM_LIMIT),
        name="diff_attention",
    )(lam_vec, head_g, q, k, vt)


def _ffn_kernel(x_ref, pool_ref, attn_ref, wo_ref, gf_ref, wup_ref, cw_ref, cb_ref, wd_ref,
                gl_ref, o_ref, carry_ref, x1_ref, h_ref, a_ref, u00_ref, u01_ref, u10_ref, u11_ref,
                *, tm, slab, chunk, tiles_per_seq, n_chunks, apply_final):
    u_refs = ((u00_ref, u01_ref), (u10_ref, u11_ref))
    seq_tile = pl.program_id(0) % tiles_per_seq

    @pl.when(seq_tile == 0)
    def _():
        carry_ref[...] = jnp.zeros_like(carry_ref)

    mixed = jnp.concatenate([pool_ref[...], attn_ref[...]], axis=1)
    x1 = x_ref[...] + _dot(mixed, wo_ref[...])
    x1_ref[...] = x1
    ms = jnp.mean(x1 * x1, axis=-1, keepdims=True)
    h_ref[...] = ((x1 * lax.rsqrt(ms + EPS)) * gf_ref[...]).astype(BF16)

    n_slabs = tm // slab
    cols = lambda idx: slice(idx * chunk, (idx + 1) * chunk)

    def up_matmul(c, r):
        h = h_ref[r * slab:(r + 1) * slab, :]
        return [_dot(h, wup_ref[:, cols(idx)]) for idx in (c, n_chunks + c)]

    def up_store(c, slot, r, us):
        for br, idx in enumerate((c, n_chunks + c)):
            u_ref = u_refs[slot][br]
            if r == 0:
                u_ref[0:CONV_HALO, :] = carry_ref[:, cols(idx)]
            u_ref[CONV_HALO + r * slab:CONV_HALO + (r + 1) * slab, :] = us[br]
            if r == n_slabs - 1:
                carry_ref[:, cols(idx)] = u_ref[tm:, :]

    def conv_branch(slot, br, idx, r):
        r0 = CONV_HALO + r * slab
        u_ref = u_refs[slot][br]
        ext = u_ref[r0 - SUBLANES:r0 + slab, :].reshape(slab // SUBLANES + 1, SUBLANES, chunk)
        sub = lax.broadcasted_iota(jnp.int32, (1, SUBLANES, chunk), 1)

        def delay_one(a):
            rot = pltpu.roll(a, 1, axis=1)
            return jnp.concatenate([rot[:1], jnp.where(sub < 1, rot[:-1], rot[1:])], axis=0)

        d1 = delay_one(ext)
        d2 = delay_one(d1)
        u = ext[1:].reshape(slab, chunk)
        u1 = d1[1:].reshape(slab, chunk)
        u2 = d2[1:].reshape(slab, chunk)
        cw = cw_ref[:, cols(idx)]
        return ((cb_ref[:, cols(idx)] + cw[0:1] * u2) + cw[1:2] * u1) + cw[2:3] * u

    for r in range(n_slabs):
        up_store(0, 0, r, up_matmul(0, r))
    for c in range(n_chunks):
        slot = c % 2
        for r in range(n_slabs):
            nxt = up_matmul(c + 1, r) if c + 1 < n_chunks else None
            gate = conv_branch(slot, 0, c, r)
            val = conv_branch(slot, 1, n_chunks + c, r)
            act = (gate * jax.nn.sigmoid(gate)) * val
            a_ref[r * slab:(r + 1) * slab, cols(c)] = act.astype(BF16)
            if nxt is not None:
                up_store(c + 1, 1 - slot, r, nxt)
    ffn_out = _dot(a_ref[...], wd_ref[...])

    x2 = x1_ref[...] + ffn_out
    if apply_final:
        ms2 = jnp.mean(x2 * x2, axis=-1, keepdims=True)
        x2 = (x2 * lax.rsqrt(ms2 + EPS)) * gl_ref[...]
    o_ref[...] = x2


def _ffn(x2d, pool_out, attn_out, wo, g_ffn, wup, cw, cb, wd, g_final,
         *, seq, tm, chunk, apply_final):
    T, d_model = x2d.shape
    n_chunks = wd.shape[0] // chunk
    kern = functools.partial(_ffn_kernel, tm=tm, tiles_per_seq=seq // tm,
                             slab=MXU_DIM, chunk=chunk, n_chunks=n_chunks, apply_final=apply_final)
    row = lambda i: (i, 0)
    const2 = lambda i: (0, 0)
    resident = dict(pipeline_mode=pl.Buffered(1))
    return pl.pallas_call(
        kern,
        grid=(T // tm,),
        in_specs=[
            pl.BlockSpec((tm, d_model), row),
            pl.BlockSpec((tm, pool_out.shape[1]), row),
            pl.BlockSpec((tm, attn_out.shape[1]), row),
            pl.BlockSpec(wo.shape, const2, **resident),
            pl.BlockSpec((1, d_model), const2),
            pl.BlockSpec(wup.shape, const2, **resident),
            pl.BlockSpec(cw.shape, const2),
            pl.BlockSpec(cb.shape, const2),
            pl.BlockSpec(wd.shape, const2, **resident),
            pl.BlockSpec((1, d_model), const2),
        ],
        out_specs=pl.BlockSpec((tm, d_model), row),
        out_shape=jax.ShapeDtypeStruct((T, d_model), F32),
        scratch_shapes=[
            pltpu.VMEM((CONV_HALO, 2 * n_chunks * chunk), F32),
            pltpu.VMEM((tm, d_model), F32),
            pltpu.VMEM((tm, d_model), BF16),
            pltpu.VMEM((tm, n_chunks * chunk), BF16),
        ] + [pltpu.VMEM((CONV_HALO + tm, chunk), F32)] * 4,
        compiler_params=pltpu.CompilerParams(
            dimension_semantics=("arbitrary",),
            vmem_limit_bytes=FFN_VMEM_LIMIT),
        name="outproj_convffn",
    )(x2d, pool_out, attn_out, wo, g_ffn, wup, cw, cb, wd, g_final)


def kernel(x, positions, norm_mix_g, w_in, pool_w, pool_scale, lambda_q1, lambda_k1, lambda_q2,
           lambda_k2, attn_norm_g, w_o, norm_ffn_g, w_up, conv_w, conv_b, w_down, norm_final_g):
    batch, seq, d_model = x.shape
    depth = w_in.shape[0]
    T = batch * seq
    d_ff = w_down.shape[1]
    n_heads = attn_norm_g.shape[1]
    chunk = MXU_DIM
    assert d_ff % chunk == 0 and seq % ROW_TILE == 0

    x2d = x.reshape(T, d_model)
    half = DIFF_HEAD_DIM // 2
    groups = LANES // half
    pos_c = positions.astype(F32).reshape(T // ROW_TILE, groups, ROW_TILE // groups).transpose(0, 2, 1)
    pos_c = jnp.broadcast_to(pos_c[..., None], pos_c.shape + (half,)).reshape(T // groups, LANES)
    inv_freq = ROPE_THETA ** (-jnp.arange(0, DIFF_HEAD_DIM, 2, dtype=F32) / DIFF_HEAD_DIM)
    invf = jnp.tile(inv_freq, groups).reshape(1, LANES)

    for l in range(depth):
        lam_init = 0.8 - 0.6 * float(np.exp(-0.3 * l))
        pool_out, q, k, vt = _inproj(
            x2d, pos_c, invf, norm_mix_g[l].reshape(1, d_model), w_in[l].astype(BF16),
            pool_w[l].astype(BF16), pool_scale[l].reshape(1, -1), seq=seq, tm=ROW_TILE)
        lam_vec = jnp.stack([lambda_q1[l], lambda_k1[l], lambda_q2[l], lambda_k2[l]]).astype(F32)
        attn_out = _diff_attention(
            q, k, vt, lam_vec, attn_norm_g[l].reshape(n_heads, 1, V_HEAD_DIM),
            batch=batch, seq=seq, lam_init=lam_init, tq=ROW_TILE, heads=n_heads)
        x2d = _ffn(
            x2d, pool_out, attn_out, w_o[l].astype(BF16), norm_ffn_g[l].reshape(1, d_model),
            w_up[l].astype(BF16), conv_w[l], conv_b[l].reshape(1, -1), w_down[l].astype(BF16),
            norm_final_g.reshape(1, d_model),
            seq=seq, tm=ROW_TILE, chunk=chunk, apply_final=(l == depth - 1))
    return x2d.reshape(batch, seq, d_model)
```

```python
import functools

import numpy as np
import jax
import jax.numpy as jnp
from jax import lax
from jax.experimental import pallas as pl
from jax.experimental.pallas import tpu as pltpu

CHUNK = 64
POOL_WINDOWS = (2, 4, 8, 16)
POOL_GROUP_DIM = 128
DIFF_HEAD_DIM = 64
V_HEAD_DIM = 2 * DIFF_HEAD_DIM
CONV_WIDTH = 3
ROPE_THETA = 10000.0
EPS = 1e-6
LOG2_E = 1.4426950408889634

LANES = 128
SUBLANES = 8
MXU_DIM = 256

POOL_HALO = 16
CONV_HALO = SUBLANES
ONES_ROWS = 16
ACC_ROWS = V_HEAD_DIM + ONES_ROWS
MAX_WAYS = 8

ROW_TILE = 512

MIB = 1024 * 1024
INPROJ_VMEM_LIMIT = 48 * MIB
ATTN_VMEM_LIMIT = 52 * MIB
FFN_VMEM_LIMIT = 56 * MIB

BF16 = jnp.bfloat16
F32 = jnp.float32


def _dot(a, b):
    return jnp.dot(a, b, preferred_element_type=F32)


def _shift_rows(a, k):
    rows, width = a.shape
    a3 = a.reshape(rows // SUBLANES, SUBLANES, width)
    above = lambda x: jnp.concatenate([x[:1], x[:-1]], axis=0)
    if k == SUBLANES:
        return above(a3).reshape(rows, width)
    rot = pltpu.roll(a3, k, axis=1)
    sub = lax.broadcasted_iota(jnp.int32, (1, SUBLANES, width), 1)
    return jnp.where(sub < k, above(rot), rot).reshape(rows, width)


def _inproj_kernel(x_ref, pos_ref, invf_ref, g_ref, w_ref, pw_ref, ps_ref,
                   pool_ref, q_ref, k_ref, v_ref, carry_ref, *, tm, tiles_per_seq,
                   pool_width, attn_width):
    seq_tile = pl.program_id(0) % tiles_per_seq

    @pl.when(seq_tile == 0)
    def _():
        carry_ref[...] = jnp.zeros_like(carry_ref)

    x = x_ref[...]
    ms = jnp.mean(x * x, axis=-1, keepdims=True)
    h = ((x * lax.rsqrt(ms + EPS)) * g_ref[...]).astype(BF16)

    q0 = pool_width
    k0 = pool_width + attn_width
    v0 = pool_width + 2 * attn_width
    p = _dot(h, w_ref[:, 0:pool_width])
    q = _dot(h, w_ref[:, q0:k0])
    k = _dot(h, w_ref[:, k0:v0])
    v = _dot(h, w_ref[:, v0:v0 + attn_width])

    t = seq_tile * tm + lax.broadcasted_iota(jnp.int32, (tm, POOL_GROUP_DIM), 0)
    pooled_out = []
    for g, w in enumerate(POOL_WINDOWS):
        cols = slice(g * POOL_GROUP_DIM, (g + 1) * POOL_GROUP_DIM)
        pg = p[:, cols]
        s = jnp.concatenate([carry_ref[:, cols], pg], axis=0)
        span = 1
        while span < w:
            s = s + _shift_rows(s, span)
            span *= 2
        count = jnp.minimum(t + 1, w).astype(F32)
        d = (s[POOL_HALO:] / count - pg).astype(BF16)
        pooled_out.append(_dot(d, pw_ref[g]))
    carry_ref[...] = p[tm - POOL_HALO:, :]
    pool_ref[...] = (jnp.concatenate(pooled_out, axis=1) * ps_ref[...]).astype(pool_ref.dtype)

    half = DIFF_HEAD_DIM // 2
    groups = LANES // half
    tq = tm // groups
    ang = pos_ref[...] * invf_ref[...]
    cos_c = jnp.cos(ang)
    sin_c = jnp.sin(ang)
    lane = lax.broadcasted_iota(jnp.int32, (tq, LANES), 1)
    lane_group = lane // half
    first_half = (lane % DIFF_HEAD_DIM) < half
    sign = jnp.where(first_half, -1.0, 1.0).astype(F32)

    def spread(x, a):
        y = None
        for g in range(groups):
            shift = ((g - a) * half) % LANES
            xr = x if shift == 0 else pltpu.roll(x, shift, axis=1)
            y = xr if y is None else jnp.where(lane_group == g, xr, y)
        return y

    tables = [(spread(cos_c, a), spread(sin_c, a) * sign) for a in range(groups)]

    def rope(tt, scale):
        blocks = []
        for a, (cos, sin) in enumerate(tables):
            outs = []
            for j in range(attn_width // LANES):
                tj = tt[a * tq:(a + 1) * tq, j * LANES:(j + 1) * LANES]
                partner = jnp.where(first_half,
                                    pltpu.roll(tj, LANES - half, axis=1),
                                    pltpu.roll(tj, half, axis=1))
                outs.append((tj * cos + partner * sin) * scale)
            blocks.append(jnp.concatenate(outs, axis=1))
        return jnp.concatenate(blocks, axis=0)

    q_ref[...] = rope(q, DIFF_HEAD_DIM ** -0.5 * LOG2_E).astype(q_ref.dtype)
    k_ref[...] = rope(k, 1.0).astype(k_ref.dtype)
    v_ref[0] = v.T.astype(v_ref.dtype)


def _inproj(x2, pos_b, invf, g, w, pool_w, pool_scale, *, seq, tm):
    T, d_model = x2.shape
    pool_width = pool_scale.shape[-1]
    attn_width = (w.shape[1] - pool_width) // 3
    n_groups = pool_w.shape[0]
    kern = functools.partial(_inproj_kernel, tm=tm, tiles_per_seq=seq // tm,
                             pool_width=pool_width, attn_width=attn_width)
    row = lambda i: (i, 0)
    const2 = lambda i: (0, 0)
    out_sds = lambda width: jax.ShapeDtypeStruct((T, width), BF16)
    return pl.pallas_call(
        kern,
        grid=(T // tm,),
        in_specs=[
            pl.BlockSpec((tm, d_model), row),
            pl.BlockSpec((tm * (DIFF_HEAD_DIM // 2) // LANES, LANES), row),
            pl.BlockSpec((1, LANES), const2),
            pl.BlockSpec((1, d_model), const2),
            pl.BlockSpec(w.shape, const2),
            pl.BlockSpec(pool_w.shape, lambda i: (0, 0, 0)),
            pl.BlockSpec((1, pool_width), const2),
        ],
        out_specs=[
            pl.BlockSpec((tm, pool_width), row),
            pl.BlockSpec((tm, attn_width), row),
            pl.BlockSpec((tm, attn_width), row),
            pl.BlockSpec((1, attn_width, tm), lambda i: (i, 0, 0)),
        ],
        out_shape=[out_sds(pool_width), out_sds(attn_width), out_sds(attn_width),
                   jax.ShapeDtypeStruct((T // tm, attn_width, tm), BF16)],
        scratch_shapes=[pltpu.VMEM((POOL_HALO, n_groups * POOL_GROUP_DIM), F32)],
        compiler_params=pltpu.CompilerParams(
            dimension_semantics=("arbitrary",),
            vmem_limit_bytes=INPROJ_VMEM_LIMIT),
        name="inproj_pool_rope",
    )(x2, pos_b, invf, g, w, pool_w, pool_scale)


def _attn_kernel(lam_ref, g_ref, q_ref, k_ref, vt_ref, o_ref, m_ref, acc_ref, sa_ref, sb_ref,
                 ma_ref, mb_ref,
                 *, tq, tk, heads, lam_init):
    i = pl.program_id(2)
    cols = 2 * tq
    groups = tk // SUBLANES

    qqts = []
    buf_a, buf_b = (sa_ref, ma_ref), (sb_ref, mb_ref)

    def put_scores(buf, hh, s):
        s4 = s.reshape(MAX_WAYS, groups // MAX_WAYS, SUBLANES, cols)
        buf[0][hh] = s
        buf[1][hh] = jnp.max(jnp.max(jnp.max(s4, axis=1), axis=0), axis=0, keepdims=True)

    def scores(hh, j):
        ch = slice(hh * V_HEAD_DIM, (hh + 1) * V_HEAD_DIM)
        kt = k_ref[pl.ds(pl.multiple_of(j * tk, tk), tk), ch]
        return _dot(kt, qqts[hh])

    def prologue():
        m_ref[...] = jnp.full(m_ref.shape, -jnp.inf, F32)
        acc_ref[...] = jnp.zeros(acc_ref.shape, F32)
        qqts.clear()
        sub = lax.broadcasted_iota(jnp.int32, (V_HEAD_DIM, tq), 0)
        for hh in range(heads):
            qt = q_ref[:, hh * V_HEAD_DIM:(hh + 1) * V_HEAD_DIM].astype(F32).T
            zero = jnp.zeros_like(qt)
            qqts.append(jnp.concatenate([jnp.where(sub < DIFF_HEAD_DIM, qt, zero),
                                         jnp.where(sub >= DIFF_HEAD_DIM, qt, zero)],
                                        axis=1).astype(BF16))
            put_scores(buf_a, hh, scores(hh, 0))

    def softmax_pv(hh, j, s, m_tile, masked):
        ch = slice(hh * V_HEAD_DIM, (hh + 1) * V_HEAD_DIM)
        m_prev = m_ref[hh]
        if masked:
            lane = lax.broadcasted_iota(jnp.int32, (CHUNK, LANES), 1)
            m_blocks, p_blocks = [], []
            for blk_i in range(cols // LANES):
                cs = slice(blk_i * LANES, (blk_i + 1) * LANES)
                live = CHUNK * (2 * (blk_i % (tq // LANES)) + 2)
                body = s[0:live - CHUNK, cs]
                tail = jnp.where(lane < CHUNK, -jnp.inf, s[live - CHUNK:live, cs])
                sb = jnp.concatenate([body, tail], axis=0).reshape(live // SUBLANES, SUBLANES, LANES)
                m_blk = jnp.maximum(m_prev[:, cs], jnp.max(jnp.max(sb, axis=0), axis=0, keepdims=True))
                p_blk = jnp.exp2(sb - m_blk[None]).reshape(live, LANES).astype(BF16)
                if live < tk:
                    p_blk = jnp.concatenate([p_blk, jnp.zeros((tk - live, LANES), BF16)], axis=0)
                m_blocks.append(m_blk)
                p_blocks.append(p_blk)
            m_new = jnp.concatenate(m_blocks, axis=1)
            p = jnp.concatenate(p_blocks, axis=1)
        else:
            m_new = jnp.maximum(m_prev, m_tile)
            p = jnp.exp2(s.reshape(groups, SUBLANES, cols) - m_new[None]).reshape(tk, cols).astype(BF16)
        alpha = jnp.exp2(m_prev - m_new)
        lhs = jnp.concatenate([vt_ref[j, ch, :], jnp.ones((ONES_ROWS, tk), BF16)], axis=0)
        pv = _dot(lhs, p)
        acc3 = acc_ref[hh].reshape(ACC_ROWS // SUBLANES, SUBLANES, cols)
        acc_ref[hh] = (alpha[None] * acc3).reshape(ACC_ROWS, cols) + pv
        m_ref[hh] = m_new

    def store_scores(j, buf):
        for hh in range(heads):
            put_scores(buf, hh, scores(hh, j))

    def full_step(j, cur, nxt):
        lead = min(1, heads)
        for hh in range(lead):
            put_scores(nxt, hh, scores(hh, j + 1))
        for hh in range(heads):
            softmax_pv(hh, j, cur[0][hh], cur[1][hh], masked=False)
            if hh + lead < heads:
                put_scores(nxt, hh + lead, scores(hh + lead, j + 1))

    def last_step(j, cur):
        for hh in range(heads):
            softmax_pv(hh, j, cur[0][hh], None, masked=True)

    def finalize():
        lam_vec = lam_ref[...]
        lam = (jnp.exp(jnp.sum(lam_vec[0:1] * lam_vec[1:2], axis=1, keepdims=True))
               - jnp.exp(jnp.sum(lam_vec[2:3] * lam_vec[3:4], axis=1, keepdims=True))
               + lam_init)
        for hh in range(heads):
            acc = acc_ref[hh]
            ot = acc[:V_HEAD_DIM] / acc[V_HEAD_DIM:V_HEAD_DIM + 1]
            od = (ot[:, :tq] - lam * ot[:, tq:]).T
            y = od * lax.rsqrt(jnp.mean(od * od, axis=-1, keepdims=True) + EPS)
            o_ref[:, hh * V_HEAD_DIM:(hh + 1) * V_HEAD_DIM] = (
                (y * g_ref[hh]) * (1.0 - lam_init)).astype(o_ref.dtype)

    n_full = (i * tq) // tk

    @pl.when(n_full == 0)
    def _():
        prologue()
        last_step(0, buf_a)
        finalize()

    @pl.when(n_full > 0)
    def _():
        prologue()
        full_step(0, buf_a, buf_b)
        rest = n_full - 1

        def pair(t, carry):
            full_step(2 * t + 1, buf_b, buf_a)
            full_step(2 * t + 2, buf_a, buf_b)
            return carry

        lax.fori_loop(0, rest // 2, pair, 0)

        @pl.when(rest % 2 == 1)
        def _():
            full_step(n_full - 1, buf_b, buf_a)
            last_step(n_full, buf_a)
            finalize()

        @pl.when(rest % 2 == 0)
        def _():
            last_step(n_full, buf_b)
            finalize()


def _diff_attention(q, k, vt, lam_vec, head_g, *, batch, seq, lam_init, tq, heads):
    T, attn_width = q.shape
    tk = vt.shape[2]
    assert tq == tk and 2 * CHUNK == LANES and tq % LANES == 0
    n_heads = attn_width // V_HEAD_DIM
    nq = seq // tq
    nk = seq // tk
    width = heads * V_HEAD_DIM
    kern = functools.partial(_attn_kernel, tq=tq, tk=tk, heads=heads, lam_init=lam_init)
    q_spec = pl.BlockSpec((tq, width), lambda b, h, i: (b * nq + i, h))
    return pl.pallas_call(
        kern,
        grid=(batch, n_heads // heads, nq),
        in_specs=[
            pl.BlockSpec(lam_vec.shape, lambda b, h, i: (0, 0)),
            pl.BlockSpec((heads, 1, V_HEAD_DIM), lambda b, h, i: (h, 0, 0)),
            q_spec,
            pl.BlockSpec((seq, width), lambda b, h, i: (b, h)),
            pl.BlockSpec((nk, width, tk), lambda b, h, i: (b, h, 0)),
        ],
        out_specs=q_spec,
        out_shape=jax.ShapeDtypeStruct((T, attn_width), BF16),
        scratch_shapes=[
            pltpu.VMEM((heads, SUBLANES, 2 * tq), F32),
            pltpu.VMEM((heads, ACC_ROWS, 2 * tq), F32),
            pltpu.VMEM((heads, tk, 2 * tq), F32),
            pltpu.VMEM((heads, tk, 2 * tq), F32),
            pltpu.VMEM((heads, 1, 2 * tq), F32),
            pltpu.VMEM((heads, 1, 2 * tq), F32),
        ],
        compiler_params=pltpu.CompilerParams(
            dimension_semantics=("arbitrary", "arbitrary", "arbitrary"),
            vmem_limit_bytes=ATTN_VMEM_LIMIT),
        name="diff_attention",
    )(lam_vec, head_g, q, k, vt)


def _ffn_kernel(x_ref, pool_ref, attn_ref, wo_ref, gf_ref, wup_ref, cw_ref, cb_ref, wd_ref,
                gl_ref, o_ref, carry_ref, x1_ref, h_ref, a_ref, u00_ref, u01_ref, u10_ref, u11_ref,
                *, tm, slab, chunk, tiles_per_seq, n_chunks, apply_final):
    u_refs = ((u00_ref, u01_ref), (u10_ref, u11_ref))
    seq_tile = pl.program_id(0) % tiles_per_seq

    @pl.when(seq_tile == 0)
    def _():
        carry_ref[...] = jnp.zeros_like(carry_ref)

    mixed = jnp.concatenate([pool_ref[...], attn_ref[...]], axis=1)
    x1 = x_ref[...] + _dot(mixed, wo_ref[...])
    x1_ref[...] = x1
    ms = jnp.mean(x1 * x1, axis=-1, keepdims=True)
    h_ref[...] = ((x1 * lax.rsqrt(ms + EPS)) * gf_ref[...]).astype(BF16)

    n_slabs = tm // slab
    cols = lambda idx: slice(idx * chunk, (idx + 1) * chunk)

    def up_matmul(c, r):
        h = h_ref[r * slab:(r + 1) * slab, :]
        return [_dot(h, wup_ref[:, cols(idx)]) for idx in (c, n_chunks + c)]

    def up_store(c, slot, r, us):
        for br, idx in enumerate((c, n_chunks + c)):
            u_ref = u_refs[slot][br]
            if r == 0:
                u_ref[0:CONV_HALO, :] = carry_ref[:, cols(idx)]
            u_ref[CONV_HALO + r * slab:CONV_HALO + (r + 1) * slab, :] = us[br]
            if r == n_slabs - 1:
                carry_ref[:, cols(idx)] = u_ref[tm:, :]

    def conv_branch(slot, br, idx, r):
        r0 = CONV_HALO + r * slab
        u_ref = u_refs[slot][br]
        ext = u_ref[r0 - SUBLANES:r0 + slab, :].reshape(slab // SUBLANES + 1, SUBLANES, chunk)
        sub = lax.broadcasted_iota(jnp.int32, (1, SUBLANES, chunk), 1)

        def delay_one(a):
            rot = pltpu.roll(a, 1, axis=1)
            return jnp.concatenate([rot[:1], jnp.where(sub < 1, rot[:-1], rot[1:])], axis=0)

        d1 = delay_one(ext)
        d2 = delay_one(d1)
        u = ext[1:].reshape(slab, chunk)
        u1 = d1[1:].reshape(slab, chunk)
        u2 = d2[1:].reshape(slab, chunk)
        cw = cw_ref[:, cols(idx)]
        return ((cb_ref[:, cols(idx)] + cw[0:1] * u2) + cw[1:2] * u1) + cw[2:3] * u

    for r in range(n_slabs):
        up_store(0, 0, r, up_matmul(0, r))
    for c in range(n_chunks):
        slot = c % 2
        for r in range(n_slabs):
            nxt = up_matmul(c + 1, r) if c + 1 < n_chunks else None
            gate = conv_branch(slot, 0, c, r)
            val = conv_branch(slot, 1, n_chunks + c, r)
            act = (gate * jax.nn.sigmoid(gate)) * val
            a_ref[r * slab:(r + 1) * slab, cols(c)] = act.astype(BF16)
            if nxt is not None:
                up_store(c + 1, 1 - slot, r, nxt)
    ffn_out = _dot(a_ref[...], wd_ref[...])

    x2 = x1_ref[...] + ffn_out
    if apply_final:
        ms2 = jnp.mean(x2 * x2, axis=-1, keepdims=True)
        x2 = (x2 * lax.rsqrt(ms2 + EPS)) * gl_ref[...]
    o_ref[...] = x2


def _ffn(x2d, pool_out, attn_out, wo, g_ffn, wup, cw, cb, wd, g_final,
         *, seq, tm, chunk, apply_final):
    T, d_model = x2d.shape
    n_chunks = wd.shape[0] // chunk
    kern = functools.partial(_ffn_kernel, tm=tm, tiles_per_seq=seq // tm,
                             slab=MXU_DIM, chunk=chunk, n_chunks=n_chunks, apply_final=apply_final)
    row = lambda i: (i, 0)
    const2 = lambda i: (0, 0)
    resident = dict(pipeline_mode=pl.Buffered(1))
    return pl.pallas_call(
        kern,
        grid=(T // tm,),
        in_specs=[
            pl.BlockSpec((tm, d_model), row),
            pl.BlockSpec((tm, pool_out.shape[1]), row),
            pl.BlockSpec((tm, attn_out.shape[1]), row),
            pl.BlockSpec(wo.shape, const2, **resident),
            pl.BlockSpec((1, d_model), const2),
            pl.BlockSpec(wup.shape, const2, **resident),
            pl.BlockSpec(cw.shape, const2),
            pl.BlockSpec(cb.shape, const2),
            pl.BlockSpec(wd.shape, const2, **resident),
            pl.BlockSpec((1, d_model), const2),
        ],
        out_specs=pl.BlockSpec((tm, d_model), row),
        out_shape=jax.ShapeDtypeStruct((T, d_model), F32),
        scratch_shapes=[
            pltpu.VMEM((CONV_HALO, 2 * n_chunks * chunk), F32),
            pltpu.VMEM((tm, d_model), F32),
            pltpu.VMEM((tm, d_model), BF16),
            pltpu.VMEM((tm, n_chunks * chunk), BF16),
        ] + [pltpu.VMEM((CONV_HALO + tm, chunk), F32)] * 4,
        compiler_params=pltpu.CompilerParams(
            dimension_semantics=("arbitrary",),
            vmem_limit_bytes=FFN_VMEM_LIMIT),
        name="outproj_convffn",
    )(x2d, pool_out, attn_out, wo, g_ffn, wup, cw, cb, wd, g_final)


def kernel(x, positions, norm_mix_g, w_in, pool_w, pool_scale, lambda_q1, lambda_k1, lambda_q2,
           lambda_k2, attn_norm_g, w_o, norm_ffn_g, w_up, conv_w, conv_b, w_down, norm_final_g):
    batch, seq, d_model = x.shape
    depth = w_in.shape[0]
    T = batch * seq
    d_ff = w_down.shape[1]
    n_heads = attn_norm_g.shape[1]
    chunk = MXU_DIM
    assert d_ff % chunk == 0 and seq % ROW_TILE == 0

    x2d = x.reshape(T, d_model)
    half = DIFF_HEAD_DIM // 2
    groups = LANES // half
    pos_c = positions.astype(F32).reshape(T // ROW_TILE, groups, ROW_TILE // groups).transpose(0, 2, 1)
    pos_c = jnp.broadcast_to(pos_c[..., None], pos_c.shape + (half,)).reshape(T // groups, LANES)
    inv_freq = ROPE_THETA ** (-jnp.arange(0, DIFF_HEAD_DIM, 2, dtype=F32) / DIFF_HEAD_DIM)
    invf = jnp.tile(inv_freq, groups).reshape(1, LANES)

    for l in range(depth):
        lam_init = 0.8 - 0.6 * float(np.exp(-0.3 * l))
        pool_out, q, k, vt = _inproj(
            x2d, pos_c, invf, norm_mix_g[l].reshape(1, d_model), w_in[l].astype(BF16),
            pool_w[l].astype(BF16), pool_scale[l].reshape(1, -1), seq=seq, tm=ROW_TILE)
        lam_vec = jnp.stack([lambda_q1[l], lambda_k1[l], lambda_q2[l], lambda_k2[l]]).astype(F32)
        attn_out = _diff_attention(
            q, k, vt, lam_vec, attn_norm_g[l].reshape(n_heads, 1, V_HEAD_DIM),
            batch=batch, seq=seq, lam_init=lam_init, tq=ROW_TILE, heads=n_heads)
        x2d = _ffn(
            x2d, pool_out, attn_out, w_o[l].astype(BF16), norm_ffn_g[l].reshape(1, d_model),
            w_up[l].astype(BF16), conv_w[l], conv_b[l].reshape(1, -1), w_down[l].astype(BF16),
            norm_final_g.reshape(1, d_model),
            seq=seq, tm=ROW_TILE, chunk=chunk, apply_final=(l == depth - 1))
    return x2d.reshape(batch, seq, d_model)
```

```python
import functools

import numpy as np
import jax
import jax.numpy as jnp
from jax import lax
from jax.experimental import pallas as pl
from jax.experimental.pallas import tpu as pltpu

CHUNK = 64
POOL_WINDOWS = (2, 4, 8, 16)
POOL_GROUP_DIM = 128
DIFF_HEAD_DIM = 64
V_HEAD_DIM = 2 * DIFF_HEAD_DIM
CONV_WIDTH = 3
ROPE_THETA = 10000.0
EPS = 1e-6
LOG2_E = 1.4426950408889634

LANES = 128
SUBLANES = 8
MXU_DIM = 256

POOL_HALO = 16
CONV_HALO = SUBLANES
ONES_ROWS = 16
ACC_ROWS = V_HEAD_DIM + ONES_ROWS
MAX_WAYS = 8

ROW_TILE = 512

MIB = 1024 * 1024
INPROJ_VMEM_LIMIT = 48 * MIB
ATTN_VMEM_LIMIT = 52 * MIB
FFN_VMEM_LIMIT = 56 * MIB

BF16 = jnp.bfloat16
F32 = jnp.float32


def _dot(a, b):
    return jnp.dot(a, b, preferred_element_type=F32)


def _shift_rows(a, k):
    rows, width = a.shape
    a3 = a.reshape(rows // SUBLANES, SUBLANES, width)
    above = lambda x: jnp.concatenate([x[:1], x[:-1]], axis=0)
    if k == SUBLANES:
        return above(a3).reshape(rows, width)
    rot = pltpu.roll(a3, k, axis=1)
    sub = lax.broadcasted_iota(jnp.int32, (1, SUBLANES, width), 1)
    return jnp.where(sub < k, above(rot), rot).reshape(rows, width)


def _inproj_kernel(x_ref, pos_ref, invf_ref, g_ref, w_ref, pw_ref, ps_ref,
                   pool_ref, q_ref, k_ref, v_ref, carry_ref, *, tm, tiles_per_seq,
                   pool_width, attn_width):
    seq_tile = pl.program_id(0) % tiles_per_seq

    @pl.when(seq_tile == 0)
    def _():
        carry_ref[...] = jnp.zeros_like(carry_ref)

    x = x_ref[...]
    ms = jnp.mean(x * x, axis=-1, keepdims=True)
    h = ((x * lax.rsqrt(ms + EPS)) * g_ref[...]).astype(BF16)

    q0 = pool_width
    k0 = pool_width + attn_width
    v0 = pool_width + 2 * attn_width
    p = _dot(h, w_ref[:, 0:pool_width])
    q = _dot(h, w_ref[:, q0:k0])
    k = _dot(h, w_ref[:, k0:v0])
    v = _dot(h, w_ref[:, v0:v0 + attn_width])

    t = seq_tile * tm + lax.broadcasted_iota(jnp.int32, (tm, POOL_GROUP_DIM), 0)
    pooled_out = []
    for g, w in enumerate(POOL_WINDOWS):
        cols = slice(g * POOL_GROUP_DIM, (g + 1) * POOL_GROUP_DIM)
        pg = p[:, cols]
        s = jnp.concatenate([carry_ref[:, cols], pg], axis=0)
        span = 1
        while span < w:
            s = s + _shift_rows(s, span)
            span *= 2
        count = jnp.minimum(t + 1, w).astype(F32)
        d = (s[POOL_HALO:] / count - pg).astype(BF16)
        pooled_out.append(_dot(d, pw_ref[g]))
    carry_ref[...] = p[tm - POOL_HALO:, :]
    pool_ref[...] = (jnp.concatenate(pooled_out, axis=1) * ps_ref[...]).astype(pool_ref.dtype)

    half = DIFF_HEAD_DIM // 2
    groups = LANES // half
    tq = tm // groups
    ang = pos_ref[...] * invf_ref[...]
    cos_c = jnp.cos(ang)
    sin_c = jnp.sin(ang)
    lane = lax.broadcasted_iota(jnp.int32, (tq, LANES), 1)
    lane_group = lane // half
    first_half = (lane % DIFF_HEAD_DIM) < half
    sign = jnp.where(first_half, -1.0, 1.0).astype(F32)

    def spread(x, a):
        y = None
        for g in range(groups):
            shift = ((g - a) * half) % LANES
            xr = x if shift == 0 else pltpu.roll(x, shift, axis=1)
            y = xr if y is None else jnp.where(lane_group == g, xr, y)
        return y

    tables = [(spread(cos_c, a), spread(sin_c, a) * sign) for a in range(groups)]

    def rope(tt, scale):
        blocks = []
        for a, (cos, sin) in enumerate(tables):
            outs = []
            for j in range(attn_width // LANES):
                tj = tt[a * tq:(a + 1) * tq, j * LANES:(j + 1) * LANES]
                partner = jnp.where(first_half,
                                    pltpu.roll(tj, LANES - half, axis=1),
                                    pltpu.roll(tj, half, axis=1))
                outs.append((tj * cos + partner * sin) * scale)
            blocks.append(jnp.concatenate(outs, axis=1))
        return jnp.concatenate(blocks, axis=0)

    q_ref[...] = rope(q, DIFF_HEAD_DIM ** -0.5 * LOG2_E).astype(q_ref.dtype)
    k_ref[...] = rope(k, 1.0).astype(k_ref.dtype)
    v_ref[0] = v.T.astype(v_ref.dtype)


def _inproj(x2, pos_b, invf, g, w, pool_w, pool_scale, *, seq, tm):
    T, d_model = x2.shape
    pool_width = pool_scale.shape[-1]
    attn_width = (w.shape[1] - pool_width) // 3
    n_groups = pool_w.shape[0]
    kern = functools.partial(_inproj_kernel, tm=tm, tiles_per_seq=seq // tm,
                             pool_width=pool_width, attn_width=attn_width)
    row = lambda i: (i, 0)
    const2 = lambda i: (0, 0)
    out_sds = lambda width: jax.ShapeDtypeStruct((T, width), BF16)
    return pl.pallas_call(
        kern,
        grid=(T // tm,),
        in_specs=[
            pl.BlockSpec((tm, d_model), row),
            pl.BlockSpec((tm * (DIFF_HEAD_DIM // 2) // LANES, LANES), row),
            pl.BlockSpec((1, LANES), const2),
            pl.BlockSpec((1, d_model), const2),
            pl.BlockSpec(w.shape, const2),
            pl.BlockSpec(pool_w.shape, lambda i: (0, 0, 0)),
            pl.BlockSpec((1, pool_width), const2),
        ],
        out_specs=[
            pl.BlockSpec((tm, pool_width), row),
            pl.BlockSpec((tm, attn_width), row),
            pl.BlockSpec((tm, attn_width), row),
            pl.BlockSpec((1, attn_width, tm), lambda i: (i, 0, 0)),
        ],
        out_shape=[out_sds(pool_width), out_sds(attn_width), out_sds(attn_width),
                   jax.ShapeDtypeStruct((T // tm, attn_width, tm), BF16)],
        scratch_shapes=[pltpu.VMEM((POOL_HALO, n_groups * POOL_GROUP_DIM), F32)],
        compiler_params=pltpu.CompilerParams(
            dimension_semantics=("arbitrary",),
            vmem_limit_bytes=INPROJ_VMEM_LIMIT),
        name="inproj_pool_rope",
    )(x2, pos_b, invf, g, w, pool_w, pool_scale)


def _attn_kernel(lam_ref, g_ref, q_ref, k_ref, vt_ref, o_ref, m_ref, acc_ref, sa_ref, sb_ref,
                 ma_ref, mb_ref,
                 *, tq, tk, heads, lam_init):
    i = pl.program_id(2)
    cols = 2 * tq
    groups = tk // SUBLANES

    qqts = []
    buf_a, buf_b = (sa_ref, ma_ref), (sb_ref, mb_ref)

    def put_scores(buf, hh, s):
        s4 = s.reshape(MAX_WAYS, groups // MAX_WAYS, SUBLANES, cols)
        buf[0][hh] = s
        buf[1][hh] = jnp.max(jnp.max(jnp.max(s4, axis=1), axis=0), axis=0, keepdims=True)

    def scores(hh, j):
        ch = slice(hh * V_HEAD_DIM, (hh + 1) * V_HEAD_DIM)
        kt = k_ref[pl.ds(pl.multiple_of(j * tk, tk), tk), ch]
        return _dot(kt, qqts[hh])

    def prologue():
        m_ref[...] = jnp.full(m_ref.shape, -jnp.inf, F32)
        acc_ref[...] = jnp.zeros(acc_ref.shape, F32)
        qqts.clear()
        sub = lax.broadcasted_iota(jnp.int32, (V_HEAD_DIM, tq), 0)
        for hh in range(heads):
            qt = q_ref[:, hh * V_HEAD_DIM:(hh + 1) * V_HEAD_DIM].astype(F32).T
            zero = jnp.zeros_like(qt)
            qqts.append(jnp.concatenate([jnp.where(sub < DIFF_HEAD_DIM, qt, zero),
                                         jnp.where(sub >= DIFF_HEAD_DIM, qt, zero)],
                                        axis=1).astype(BF16))
            put_scores(buf_a, hh, scores(hh, 0))

    def softmax_pv(hh, j, s, m_tile, masked):
        ch = slice(hh * V_HEAD_DIM, (hh + 1) * V_HEAD_DIM)
        m_prev = m_ref[hh]
        if masked:
            lane = lax.broadcasted_iota(jnp.int32, (CHUNK, LANES), 1)
            m_blocks, p_blocks = [], []
            for blk_i in range(cols // LANES):
                cs = slice(blk_i * LANES, (blk_i + 1) * LANES)
                live = CHUNK * (2 * (blk_i % (tq // LANES)) + 2)
                body = s[0:live - CHUNK, cs]
                tail = jnp.where(lane < CHUNK, -jnp.inf, s[live - CHUNK:live, cs])
                sb = jnp.concatenate([body, tail], axis=0).reshape(live // SUBLANES, SUBLANES, LANES)
                m_blk = jnp.maximum(m_prev[:, cs], jnp.max(jnp.max(sb, axis=0), axis=0, keepdims=True))
                p_blk = jnp.exp2(sb - m_blk[None]).reshape(live, LANES).astype(BF16)
                if live < tk:
                    p_blk = jnp.concatenate([p_blk, jnp.zeros((tk - live, LANES), BF16)], axis=0)
                m_blocks.append(m_blk)
                p_blocks.append(p_blk)
            m_new = jnp.concatenate(m_blocks, axis=1)
            p = jnp.concatenate(p_blocks, axis=1)
        else:
            m_new = jnp.maximum(m_prev, m_tile)
            p = jnp.exp2(s.reshape(groups, SUBLANES, cols) - m_new[None]).reshape(tk, cols).astype(BF16)
        alpha = jnp.exp2(m_prev - m_new)
        lhs = jnp.concatenate([vt_ref[j, ch, :], jnp.ones((ONES_ROWS, tk), BF16)], axis=0)
        pv = _dot(lhs, p)
        acc3 = acc_ref[hh].reshape(ACC_ROWS // SUBLANES, SUBLANES, cols)
        acc_ref[hh] = (alpha[None] * acc3).reshape(ACC_ROWS, cols) + pv
        m_ref[hh] = m_new

    def store_scores(j, buf):
        for hh in range(heads):
            put_scores(buf, hh, scores(hh, j))

    def full_step(j, cur, nxt):
        lead = min(1, heads)
        for hh in range(lead):
            put_scores(nxt, hh, scores(hh, j + 1))
        for hh in range(heads):
            softmax_pv(hh, j, cur[0][hh], cur[1][hh], masked=False)
            if hh + lead < heads:
                put_scores(nxt, hh + lead, scores(hh + lead, j + 1))

    def last_step(j, cur):
        for hh in range(heads):
            softmax_pv(hh, j, cur[0][hh], None, masked=True)

    def finalize():
        lam_vec = lam_ref[...]
        lam = (jnp.exp(jnp.sum(lam_vec[0:1] * lam_vec[1:2], axis=1, keepdims=True))
               - jnp.exp(jnp.sum(lam_vec[2:3] * lam_vec[3:4], axis=1, keepdims=True))
               + lam_init)
        for hh in range(heads):
            acc = acc_ref[hh]
            ot = acc[:V_HEAD_DIM] / acc[V_HEAD_DIM:V_HEAD_DIM + 1]
            od = (ot[:, :tq] - lam * ot[:, tq:]).T
            y = od * lax.rsqrt(jnp.mean(od * od, axis=-1, keepdims=True) + EPS)
            o_ref[:, hh * V_HEAD_DIM:(hh + 1) * V_HEAD_DIM] = (
                (y * g_ref[hh]) * (1.0 - lam_init)).astype(o_ref.dtype)

    n_full = (i * tq) // tk

    @pl.when(n_full == 0)
    def _():
        prologue()
        last_step(0, buf_a)
        finalize()

    @pl.when(n_full > 0)
    def _():
        prologue()
        full_step(0, buf_a, buf_b)
        rest = n_full - 1

        def pair(t, carry):
            full_step(2 * t + 1, buf_b, buf_a)
            full_step(2 * t + 2, buf_a, buf_b)
            return carry

        lax.fori_loop(0, rest // 2, pair, 0)

        @pl.when(rest % 2 == 1)
        def _():
            full_step(n_full - 1, buf_b, buf_a)
            last_step(n_full, buf_a)
            finalize()

        @pl.when(rest % 2 == 0)
        def _():
            last_step(n_full, buf_b)
            finalize()


def _diff_attention(q, k, vt, lam_vec, head_g, *, batch, seq, lam_init, tq, heads):
    T, attn_width = q.shape
    tk = vt.shape[2]
    assert tq == tk and 2 * CHUNK == LANES and tq % LANES == 0
    n_heads = attn_width // V_HEAD_DIM
    nq = seq // tq
    nk = seq // tk
    width = heads * V_HEAD_DIM
    kern = functools.partial(_attn_kernel, tq=tq, tk=tk, heads=heads, lam_init=lam_init)
    q_spec = pl.BlockSpec((tq, width), lambda b, h, i: (b * nq + i, h))
    return pl.pallas_call(
        kern,
        grid=(batch, n_heads // heads, nq),
        in_specs=[
            pl.BlockSpec(lam_vec.shape, lambda b, h, i: (0, 0)),
            pl.BlockSpec((heads, 1, V_HEAD_DIM), lambda b, h, i: (h, 0, 0)),
            q_spec,
            pl.BlockSpec((seq, width), lambda b, h, i: (b, h)),
            pl.BlockSpec((nk, width, tk), lambda b, h, i: (b, h, 0)),
        ],
        out_specs=q_spec,
        out_shape=jax.ShapeDtypeStruct((T, attn_width), BF16),
        scratch_shapes=[
            pltpu.VMEM((heads, SUBLANES, 2 * tq), F32),
            pltpu.VMEM((heads, ACC_ROWS, 2 * tq), F32),
            pltpu.VMEM((heads, tk, 2 * tq), F32),
            pltpu.VMEM((heads, tk, 2 * tq), F32),
            pltpu.VMEM((heads, 1, 2 * tq), F32),
            pltpu.VMEM((heads, 1, 2 * tq), F32),
        ],
        compiler_params=pltpu.CompilerParams(
            dimension_semantics=("arbitrary", "arbitrary", "arbitrary"),
            vmem_limit_bytes=ATTN_VMEM_LIMIT),
        name="diff_attention",
    )(lam_vec, head_g, q, k, vt)


def _ffn_kernel(x_ref, pool_ref, attn_ref, wo_ref, gf_ref, wup_ref, cw_ref, cb_ref, wd_ref,
                gl_ref, o_ref, carry_ref, x1_ref, h_ref, a_ref, u00_ref, u01_ref, u10_ref, u11_ref,
                *, tm, slab, chunk, tiles_per_seq, n_chunks, apply_final):
    u_refs = ((u00_ref, u01_ref), (u10_ref, u11_ref))
    seq_tile = pl.program_id(0) % tiles_per_seq

    @pl.when(seq_tile == 0)
    def _():
        carry_ref[...] = jnp.zeros_like(carry_ref)

    mixed = jnp.concatenate([pool_ref[...], attn_ref[...]], axis=1)
    x1 = x_ref[...] + _dot(mixed, wo_ref[...])
    x1_ref[...] = x1
    ms = jnp.mean(x1 * x1, axis=-1, keepdims=True)
    h_ref[...] = ((x1 * lax.rsqrt(ms + EPS)) * gf_ref[...]).astype(BF16)

    n_slabs = tm // slab
    cols = lambda idx: slice(idx * chunk, (idx + 1) * chunk)

    def up_matmul(c, r):
        h = h_ref[r * slab:(r + 1) * slab, :]
        return [_dot(h, wup_ref[:, cols(idx)]) for idx in (c, n_chunks + c)]

    def up_store(c, slot, r, us):
        for br, idx in enumerate((c, n_chunks + c)):
            u_ref = u_refs[slot][br]
            if r == 0:
                u_ref[0:CONV_HALO, :] = carry_ref[:, cols(idx)]
            u_ref[CONV_HALO + r * slab:CONV_HALO + (r + 1) * slab, :] = us[br]
            if r == n_slabs - 1:
                carry_ref[:, cols(idx)] = u_ref[tm:, :]

    def conv_branch(slot, br, idx, r, lanes):
        r0 = CONV_HALO + r * slab
        u_ref = u_refs[slot][br]
        width = lanes.stop - lanes.start
        ext = u_ref[r0 - SUBLANES:r0 + slab, lanes].reshape(slab // SUBLANES + 1, SUBLANES, width)
        sub = lax.broadcasted_iota(jnp.int32, (1, SUBLANES, width), 1)

        def delay_one(a):
            rot = pltpu.roll(a, 1, axis=1)
            return jnp.concatenate([rot[:1], jnp.where(sub < 1, rot[:-1], rot[1:])], axis=0)

        d1 = delay_one(ext)
        d2 = delay_one(d1)
        u = ext[1:].reshape(slab, width)
        u1 = d1[1:].reshape(slab, width)
        u2 = d2[1:].reshape(slab, width)
        wcols = slice(idx * chunk + lanes.start, idx * chunk + lanes.stop)
        cw = cw_ref[:, wcols]
        return ((cb_ref[:, wcols] + cw[0:1] * u2) + cw[1:2] * u1) + cw[2:3] * u

    for r in range(n_slabs):
        up_store(0, 0, r, up_matmul(0, r))
    for c in range(n_chunks):
        slot = c % 2
        for r in range(n_slabs):
            nxt = up_matmul(c + 1, r) if c + 1 < n_chunks else None
            for lane0 in range(0, chunk, LANES):
                lanes = slice(lane0, lane0 + LANES)
                gate = conv_branch(slot, 0, c, r, lanes)
                val = conv_branch(slot, 1, n_chunks + c, r, lanes)
                act = (gate * jax.nn.sigmoid(gate)) * val
                a_ref[r * slab:(r + 1) * slab, c * chunk + lane0:c * chunk + lane0 + LANES] = act.astype(BF16)
            if nxt is not None:
                up_store(c + 1, 1 - slot, r, nxt)
    ffn_out = _dot(a_ref[...], wd_ref[...])

    x2 = x1_ref[...] + ffn_out
    if apply_final:
        ms2 = jnp.mean(x2 * x2, axis=-1, keepdims=True)
        x2 = (x2 * lax.rsqrt(ms2 + EPS)) * gl_ref[...]
    o_ref[...] = x2


def _ffn(x2d, pool_out, attn_out, wo, g_ffn, wup, cw, cb, wd, g_final,
         *, seq, tm, chunk, apply_final):
    T, d_model = x2d.shape
    n_chunks = wd.shape[0] // chunk
    kern = functools.partial(_ffn_kernel, tm=tm, tiles_per_seq=seq // tm,
                             slab=MXU_DIM, chunk=chunk, n_chunks=n_chunks, apply_final=apply_final)
    row = lambda i: (i, 0)
    const2 = lambda i: (0, 0)
    resident = dict(pipeline_mode=pl.Buffered(1))
    return pl.pallas_call(
        kern,
        grid=(T // tm,),
        in_specs=[
            pl.BlockSpec((tm, d_model), row),
            pl.BlockSpec((tm, pool_out.shape[1]), row),
            pl.BlockSpec((tm, attn_out.shape[1]), row),
            pl.BlockSpec(wo.shape, const2, **resident),
            pl.BlockSpec((1, d_model), const2),
            pl.BlockSpec(wup.shape, const2, **resident),
            pl.BlockSpec(cw.shape, const2),
            pl.BlockSpec(cb.shape, const2),
            pl.BlockSpec(wd.shape, const2, **resident),
            pl.BlockSpec((1, d_model), const2),
        ],
        out_specs=pl.BlockSpec((tm, d_model), row),
        out_shape=jax.ShapeDtypeStruct((T, d_model), F32),
        scratch_shapes=[
            pltpu.VMEM((CONV_HALO, 2 * n_chunks * chunk), F32),
            pltpu.VMEM((tm, d_model), F32),
            pltpu.VMEM((tm, d_model), BF16),
            pltpu.VMEM((tm, n_chunks * chunk), BF16),
        ] + [pltpu.VMEM((CONV_HALO + tm, chunk), F32)] * 4,
        compiler_params=pltpu.CompilerParams(
            dimension_semantics=("arbitrary",),
            vmem_limit_bytes=FFN_VMEM_LIMIT),
        name="outproj_convffn",
    )(x2d, pool_out, attn_out, wo, g_ffn, wup, cw, cb, wd, g_final)


def kernel(x, positions, norm_mix_g, w_in, pool_w, pool_scale, lambda_q1, lambda_k1, lambda_q2,
           lambda_k2, attn_norm_g, w_o, norm_ffn_g, w_up, conv_w, conv_b, w_down, norm_final_g):
    batch, seq, d_model = x.shape
    depth = w_in.shape[0]
    T = batch * seq
    d_ff = w_down.shape[1]
    n_heads = attn_norm_g.shape[1]
    chunk = MXU_DIM
    assert d_ff % chunk == 0 and seq % ROW_TILE == 0

    x2d = x.reshape(T, d_model)
    half = DIFF_HEAD_DIM // 2
    groups = LANES // half
    pos_c = positions.astype(F32).reshape(T // ROW_TILE, groups, ROW_TILE // groups).transpose(0, 2, 1)
    pos_c = jnp.broadcast_to(pos_c[..., None], pos_c.shape + (half,)).reshape(T // groups, LANES)
    inv_freq = ROPE_THETA ** (-jnp.arange(0, DIFF_HEAD_DIM, 2, dtype=F32) / DIFF_HEAD_DIM)
    invf = jnp.tile(inv_freq, groups).reshape(1, LANES)

    for l in range(depth):
        lam_init = 0.8 - 0.6 * float(np.exp(-0.3 * l))
        pool_out, q, k, vt = _inproj(
            x2d, pos_c, invf, norm_mix_g[l].reshape(1, d_model), w_in[l].astype(BF16),
            pool_w[l].astype(BF16), pool_scale[l].reshape(1, -1), seq=seq, tm=ROW_TILE)
        lam_vec = jnp.stack([lambda_q1[l], lambda_k1[l], lambda_q2[l], lambda_k2[l]]).astype(F32)
        attn_out = _diff_attention(
            q, k, vt, lam_vec, attn_norm_g[l].reshape(n_heads, 1, V_HEAD_DIM),
            batch=batch, seq=seq, lam_init=lam_init, tq=ROW_TILE, heads=n_heads)
        x2d = _ffn(
            x2d, pool_out, attn_out, w_o[l].astype(BF16), norm_ffn_g[l].reshape(1, d_model),
            w_up[l].astype(BF16), conv_w[l], conv_b[l].reshape(1, -1), w_down[l].astype(BF16),
            norm_final_g.reshape(1, d_model),
            seq=seq, tm=ROW_TILE, chunk=chunk, apply_final=(l == depth - 1))
    return x2d.reshape(batch, seq, d_model)
```
